```python
import jax, jax.numpy as jnp
from jax import lax
import numpy as np

D_MODEL = 1024
BATCH = 8
SEQ = 2048
DEPTH = 1

MEM_LEN = 256
CONV_GROUPS = 8
CONV_GROUP_DIM = 64
CONV_DIM = CONV_GROUPS * CONV_GROUP_DIM
SB_HEADS = 8
SB_HEAD_DIM = 64
SB_DIM = SB_HEADS * SB_HEAD_DIM
MIX_DIM = CONV_DIM + SB_DIM
IN_PROJ_DIM = 3 * CONV_DIM + 3 * SB_DIM
CONV_K = 3
Q_BLOCK = 128
X_HEADS = 4
X_HEAD_DIM = D_MODEL // X_HEADS
N_EXPERTS = 32
TOP_K = 4
D_EXPERT = D_MODEL
SWIGLU_LIMIT = 7.0
SWIGLU_ALPHA = 1.702
EXPERT_BLOCK = 128
EPS = 1e-5

kernel_name = "hymba_conv_stickbreak_memxattn_moe"


def rmsnorm(x, g):
    x32 = x.astype(jnp.float32)
    y = x32 * lax.rsqrt(jnp.mean(x32 * x32, axis=-1, keepdims=True) + EPS)
    return (y * g.astype(jnp.float32)).astype(x.dtype)


def group_rmsnorm(y, g, n_groups):
    b, s, w = y.shape
    y32 = y.astype(jnp.float32).reshape(b, s, n_groups, w // n_groups)
    y32 = y32 * lax.rsqrt(jnp.mean(y32 * y32, axis=-1, keepdims=True) + EPS)
    return (y32.reshape(b, s, w) * g.astype(jnp.float32)).astype(y.dtype)


def short_gated_conv(b_gate, c_gate, xv, conv_w):
    u = c_gate * xv
    s = u.shape[1]
    up = jnp.pad(u, ((0, 0), (CONV_K - 1, 0), (0, 0)))
    y = conv_w[0] * up[:, 0:s]
    for i in range(1, CONV_K):
        y = y + conv_w[i] * up[:, i:i + s]
    return b_gate * y


def stick_breaking_attention(q, k, v):
    s_len = q.shape[1]
    scale = SB_HEAD_DIM ** -0.5
    outs = []
    for i in range(s_len // Q_BLOCK):
        q0 = i * Q_BLOCK
        kend = q0 + Q_BLOCK
        qb = q[:, q0:kend]
        kb = k[:, :kend]
        vb = v[:, :kend]
        z = jnp.einsum('bqhd,bkhd->bhqk', qb, kb).astype(jnp.float32) * scale
        t_idx = q0 + jnp.arange(Q_BLOCK)[:, None]
        s_idx = jnp.arange(kend)[None, :]
        valid = s_idx < t_idx
        log_keep = jnp.where(valid, jax.nn.log_sigmoid(-z), 0.0)
        rest = lax.cumsum(log_keep, axis=3, reverse=True) - log_keep
        a = jnp.where(valid, jnp.exp(jax.nn.log_sigmoid(z) + rest), 0.0)
        outs.append(jnp.einsum('bhqk,bkhd->bqhd', a.astype(v.dtype), vb))
    return jnp.concatenate(outs, axis=1)


def memory_cross_attention(h, mem_n, w_q, w_kv, w_o):
    b, s, _ = h.shape
    m = mem_n.shape[1]
    q = (h @ w_q).reshape(b, s, X_HEADS, X_HEAD_DIM)
    k, v = jnp.split(mem_n @ w_kv, 2, axis=-1)
    k = k.reshape(b, m, X_HEADS, X_HEAD_DIM)
    v = v.reshape(b, m, X_HEADS, X_HEAD_DIM)
    sc = jnp.einsum('bqhd,bkhd->bhqk', q, k).astype(jnp.float32) * (X_HEAD_DIM ** -0.5)
    p = jax.nn.softmax(sc, axis=-1).astype(v.dtype)
    o = jnp.einsum('bhqk,bkhd->bqhd', p, v).reshape(b, s, D_MODEL)
    return o @ w_o


def moe_clamped_swiglu(h, w_router, b_router, w1, b1, w2, b2):
    bn, s, d = h.shape
    n = bn * s
    hf = h.reshape(n, d)
    logits = hf.astype(jnp.float32) @ w_router.astype(jnp.float32) + b_router.astype(jnp.float32)
    top_v, top_i = lax.top_k(logits, TOP_K)
    gate = jax.nn.softmax(top_v, axis=-1)
    n_assign = n * TOP_K
    flat_e = top_i.reshape(-1)
    flat_w = gate.reshape(-1)
    flat_tok = jnp.repeat(jnp.arange(n, dtype=jnp.int32), TOP_K)
    order = jnp.argsort(flat_e)
    sorted_e = flat_e[order]
    sorted_tok = flat_tok[order]
    sorted_w = flat_w[order]
    counts = jnp.bincount(flat_e, length=N_EXPERTS)
    padded = (counts + EXPERT_BLOCK - 1) // EXPERT_BLOCK * EXPERT_BLOCK
    start_sorted = jnp.cumsum(counts) - counts
    padded_end = jnp.cumsum(padded)
    start_padded = padded_end - padded
    rows = jnp.arange(n_assign, dtype=jnp.int32)
    dest = start_padded[sorted_e] + rows - start_sorted[sorted_e]
    n_rows = ((n_assign + N_EXPERTS * EXPERT_BLOCK + EXPERT_BLOCK - 1) // EXPERT_BLOCK) * EXPERT_BLOCK
    n_blocks = n_rows // EXPERT_BLOCK
    pad_tok = jnp.zeros((n_rows,), jnp.int32).at[dest].set(sorted_tok)
    pad_w = jnp.zeros((n_rows,), jnp.float32).at[dest].set(sorted_w)
    block_start = jnp.arange(n_blocks, dtype=jnp.int32) * EXPERT_BLOCK
    block_e = jnp.minimum(jnp.searchsorted(padded_end, block_start, side='right'), N_EXPERTS - 1)

    def expert_block(args):
        e, tok, wt = args
        xb = hf[tok]
        hid = xb @ w1[e] + b1[e]
        glu, lin = jnp.split(hid, 2, axis=-1)
        glu = jnp.minimum(glu, SWIGLU_LIMIT)
        lin = jnp.clip(lin, -SWIGLU_LIMIT, SWIGLU_LIMIT)
        act = glu * jax.nn.sigmoid(SWIGLU_ALPHA * glu) * (lin + 1.0)
        out = act @ w2[e] + b2[e]
        return out * wt[:, None].astype(out.dtype)

    out_blocks = lax.map(expert_block, (block_e,
                                        pad_tok.reshape(n_blocks, EXPERT_BLOCK),
                                        pad_w.reshape(n_blocks, EXPERT_BLOCK)))
    y = jnp.zeros_like(hf).at[pad_tok].add(out_blocks.reshape(n_rows, d))
    return y.reshape(bn, s, d)


def setup_inputs(seed: int = 0) -> dict:
    key = jax.random.key(seed)
    ks = jax.random.split(key, 24)
    f32 = jnp.float32

    def nrm(k, shape, scale):
        return jax.random.normal(k, shape, f32) * scale

    def gain(k, shape):
        return 1.0 + 0.02 * jax.random.normal(k, shape, f32)

    return {
        "x": nrm(ks[0], (BATCH, SEQ, D_MODEL), 1.0),
        "mem": nrm(ks[1], (BATCH, MEM_LEN, D_MODEL), 1.0),
        "g_mix": gain(ks[2], (DEPTH, D_MODEL)),
        "w_in": nrm(ks[3], (DEPTH, D_MODEL, IN_PROJ_DIM), D_MODEL ** -0.5),
        "conv_w": nrm(ks[4], (DEPTH, CONV_K, CONV_DIM), CONV_K ** -0.5),
        "g_conv_out": gain(ks[5], (DEPTH, CONV_DIM)),
        "g_sb_out": gain(ks[6], (DEPTH, SB_DIM)),
        "w_out": nrm(ks[7], (DEPTH, MIX_DIM, D_MODEL), MIX_DIM ** -0.5),
        "g_xattn": gain(ks[8], (DEPTH, D_MODEL)),
        "g_mem": gain(ks[9], (DEPTH, D_MODEL)),
        "w_q_mem": nrm(ks[10], (DEPTH, D_MODEL, D_MODEL), D_MODEL ** -0.5),
        "w_kv_mem": nrm(ks[11], (DEPTH, D_MODEL, 2 * D_MODEL), D_MODEL ** -0.5),
        "w_o_mem": nrm(ks[12], (DEPTH, D_MODEL, D_MODEL), D_MODEL ** -0.5),
        "g_moe": gain(ks[13], (DEPTH, D_MODEL)),
        "w_router": nrm(ks[14], (DEPTH, D_MODEL, N_EXPERTS), D_MODEL ** -0.5),
        "b_router": nrm(ks[15], (DEPTH, N_EXPERTS), 0.01),
        "w1": nrm(ks[16], (DEPTH, N_EXPERTS, D_MODEL, 2 * D_EXPERT), D_MODEL ** -0.5),
        "b1": nrm(ks[17], (DEPTH, N_EXPERTS, 2 * D_EXPERT), 0.02),
        "w2": nrm(ks[18], (DEPTH, N_EXPERTS, D_EXPERT, D_MODEL), D_EXPERT ** -0.5),
        "b2": nrm(ks[19], (DEPTH, N_EXPERTS, D_MODEL), 0.02),
        "g_final": gain(ks[20], (D_MODEL,)),
    }


def reference(x, mem, g_mix, w_in, conv_w, g_conv_out, g_sb_out, w_out, g_xattn, g_mem,
              w_q_mem, w_kv_mem, w_o_mem, g_moe, w_router, b_router, w1, b1, w2, b2, g_final):
    b, s, _ = x.shape
    splits = [CONV_DIM, 2 * CONV_DIM, 3 * CONV_DIM,
              3 * CONV_DIM + SB_DIM, 3 * CONV_DIM + 2 * SB_DIM]
    for l in range(DEPTH):
        h = rmsnorm(x, g_mix[l])
        proj = h @ w_in[l]
        b_gate, c_gate, xv, q, k, v = jnp.split(proj, splits, axis=-1)
        y_conv = short_gated_conv(b_gate, c_gate, xv, conv_w[l])
        y_sb = stick_breaking_attention(q.reshape(b, s, SB_HEADS, SB_HEAD_DIM),
                                        k.reshape(b, s, SB_HEADS, SB_HEAD_DIM),
                                        v.reshape(b, s, SB_HEADS, SB_HEAD_DIM)).reshape(b, s, SB_DIM)
        y_mix = jnp.concatenate([group_rmsnorm(y_conv, g_conv_out[l], CONV_GROUPS),
                                 group_rmsnorm(y_sb, g_sb_out[l], SB_HEADS)], axis=-1)
        x = x + y_mix @ w_out[l]
        x = x + memory_cross_attention(rmsnorm(x, g_xattn[l]), rmsnorm(mem, g_mem[l]),
                                       w_q_mem[l], w_kv_mem[l], w_o_mem[l])
        x = x + moe_clamped_swiglu(rmsnorm(x, g_moe[l]), w_router[l], b_router[l],
                                   w1[l], b1[l], w2[l], b2[l])
    return rmsnorm(x, g_final)
```

```python
import jax
import jax.numpy as jnp
from jax import lax
from jax.experimental import pallas as pl
from jax.experimental.pallas import tpu as pltpu

F32 = jnp.float32
BF16 = jnp.bfloat16

EPS = 1e-5
CONV_DIM = 512
SB_DIM = 512
SB_HEAD_DIM = 64
CONV_GROUP_DIM = 64
X_HEADS = 4
N_EXPERTS = 32
TOP_K = 4
SWIGLU_LIMIT = 7.0
SWIGLU_ALPHA = 1.702

LANES = 128
ROW_TILE = 256
SB_TILE = 256
EXPERT_TILE = 256
VMEM_LIMIT = 56 * 1024 * 1024


def _rms(x, g):
    return x * lax.rsqrt(jnp.mean(x * x, axis=-1, keepdims=True) + EPS) * g


def _store_row_tiles(ref, val):
    for j in range(ref.shape[-2]):
        ref[:, j, :] = val[:, j * LANES:(j + 1) * LANES]


def _load_row_tiles(ref):
    return jnp.concatenate([ref[:, j, :] for j in range(ref.shape[-2])], axis=-1)


def _split_dot(a, m):
    hi = a.astype(BF16)
    lo = (a - hi.astype(F32)).astype(BF16)
    return (jnp.dot(hi, m, preferred_element_type=F32)
            + jnp.dot(lo, m, preferred_element_type=F32))


def _mem_kv_kernel(mem_ref, g_ref, w_ref, kv_ref):
    h = _rms(mem_ref[0], g_ref[...]).astype(BF16)
    kv_ref[0] = jnp.dot(h, w_ref[...], preferred_element_type=F32).astype(BF16)


def _mem_kv(mem, g_mem, w_kv):
    b, m, d = mem.shape
    return pl.pallas_call(
        _mem_kv_kernel,
        grid=(b,),
        in_specs=[pl.BlockSpec((1, m, d), lambda i: (i, 0, 0)),
                  pl.BlockSpec((1, d), lambda i: (0, 0)),
                  pl.BlockSpec((d, 2 * d), lambda i: (0, 0))],
        out_specs=pl.BlockSpec((1, m, 2 * d), lambda i: (i, 0, 0)),
        out_shape=jax.ShapeDtypeStruct((b, m, 2 * d), BF16),
        compiler_params=pltpu.CompilerParams(
            dimension_semantics=("arbitrary",), vmem_limit_bytes=VMEM_LIMIT),
        name="mem_kv",
    )(mem, g_mem, w_kv)


def _inproj_kernel(x_ref, g_ref, w_ref, cw_ref, gc_ref, gmat_ref,
                   yc_ref, q_ref, k_ref, v_ref, prev_ref):
    tm = x_ref.shape[1]
    h = _rms(x_ref[0], g_ref[...]).astype(BF16)
    proj = jnp.dot(h, w_ref[...], preferred_element_type=F32)
    c = CONV_DIM
    u = proj[:, c:2 * c] * proj[:, 2 * c:3 * c]

    @pl.when(pl.program_id(1) == 0)
    def _():
        prev_ref[...] = jnp.zeros_like(prev_ref)

    rows = lax.broadcasted_iota(jnp.int32, (tm, 1), 0)
    p1 = prev_ref[7:8, :]
    p2 = prev_ref[6:7, :]
    u1 = jnp.where(rows == 0, p1, pltpu.roll(u, 1, axis=0))
    u2 = jnp.where(rows == 0, p2, jnp.where(rows == 1, p1, pltpu.roll(u, 2, axis=0)))
    prev_ref[...] = u[tm - 8:tm, :]
    y = proj[:, 0:c] * (cw_ref[0:1, :] * u2 + cw_ref[1:2, :] * u1 + cw_ref[2:3, :] * u)
    ms = _split_dot(y * y, gmat_ref[...])
    yc_ref[0] = (y * lax.rsqrt(ms + EPS) * gc_ref[...]).astype(BF16)
    o = 3 * c
    q_ref[0] = (proj[:, o:o + SB_DIM] * (SB_HEAD_DIM ** -0.5)).astype(BF16)
    k_ref[0] = proj[:, o + SB_DIM:o + 2 * SB_DIM].astype(BF16)
    v_ref[0] = proj[:, o + 2 * SB_DIM:o + 3 * SB_DIM].astype(BF16)


def _inproj(x, g_mix, w_in, conv_w, g_conv, gmat):
    b, s, d = x.shape
    tm = ROW_TILE
    p = w_in.shape[1]
    row = lambda i, j: (i, j, 0)
    fixed = lambda i, j: (0, 0)
    outs = [jax.ShapeDtypeStruct((b, s, CONV_DIM), BF16)] * 4
    return pl.pallas_call(
        _inproj_kernel,
        grid=(b, s // tm),
        in_specs=[pl.BlockSpec((1, tm, d), row),
                  pl.BlockSpec((1, d), fixed),
                  pl.BlockSpec((d, p), fixed),
                  pl.BlockSpec(conv_w.shape, fixed),
                  pl.BlockSpec((1, CONV_DIM), fixed),
                  pl.BlockSpec((CONV_DIM, CONV_DIM), fixed)],
        out_specs=[pl.BlockSpec((1, tm, CONV_DIM), row)] * 4,
        out_shape=outs,
        scratch_shapes=[pltpu.VMEM((8, CONV_DIM), F32)],
        compiler_params=pltpu.CompilerParams(
            dimension_semantics=("arbitrary", "arbitrary"), vmem_limit_bytes=VMEM_LIMIT),
        name="inproj_conv",
    )(x, g_mix, w_in, conv_w, g_conv, gmat)


def _sb_kernel(q_ref, k_ref, v_ref, tri_ref, g_ref, o_ref):
    tq = q_ref.shape[1]
    tk = tq
    qi = pl.program_id(2)
    q = q_ref[0]
    lane = lax.broadcasted_iota(jnp.int32, (1, LANES), 1)
    in_a = lane < SB_HEAD_DIM
    zero = jnp.zeros_like(q)
    q_heads = (jnp.where(in_a, q, zero), jnp.where(in_a, zero, q))
    tri = tri_ref[...]

    def head_tile(qh, kb, vb, run, acc, valid):
        z = lax.dot_general(qh, kb, (((1,), (1,)), ((), ())), preferred_element_type=F32)
        sp = jnp.maximum(z, 0.0) + jnp.log(1.0 + jnp.exp(-jnp.abs(z)))
        lk = -sp
        if valid is not None:
            lk = jnp.where(valid, lk, 0.0)
        rest = _split_dot(lk, tri)
        tot = rest[:, 0:1] + lk[:, 0:1]
        a = jnp.exp(z - sp + rest + run)
        if valid is not None:
            a = jnp.where(valid, a, 0.0)
        acc = acc + jnp.dot(a.astype(BF16), vb, preferred_element_type=F32)
        return run + tot, acc

    def tile(j, carry, valid):
        start = pl.multiple_of(j * tk, tk)
        kb = k_ref[0, pl.ds(start, tk), :]
        vb = v_ref[0, pl.ds(start, tk), :]
        ra, aa = head_tile(q_heads[0], kb, vb, carry[0], carry[1], valid)
        rb, ab = head_tile(q_heads[1], kb, vb, carry[2], carry[3], valid)
        return ra, aa, rb, ab

    r0 = jnp.zeros((tq, 1), F32)
    a0 = jnp.zeros((tq, LANES), F32)
    t_idx = lax.broadcasted_iota(jnp.int32, (tq, tk), 0)
    s_idx = lax.broadcasted_iota(jnp.int32, (tq, tk), 1)
    carry = tile(qi, (r0, a0, r0, a0), s_idx < t_idx)
    carry = lax.fori_loop(0, qi, lambda n, c: tile(qi - 1 - n, c, None), carry)
    out = jnp.where(in_a, carry[1], carry[3])
    sq = out * out
    ss_a = jnp.sum(jnp.where(in_a, sq, 0.0), axis=-1, keepdims=True)
    ss_b = jnp.sum(jnp.where(in_a, 0.0, sq), axis=-1, keepdims=True)
    ms = jnp.where(in_a, ss_a, ss_b) * (1.0 / SB_HEAD_DIM)
    o_ref[0] = (out * lax.rsqrt(ms + EPS) * g_ref[...]).astype(BF16)


def _stick_breaking(q, k, v, tri, g_sb):
    b, s, w = q.shape
    tq = SB_TILE
    return pl.pallas_call(
        _sb_kernel,
        grid=(b, w // LANES, s // tq),
        in_specs=[pl.BlockSpec((1, tq, LANES), lambda i, h, j: (i, j, h)),
                  pl.BlockSpec((1, s, LANES), lambda i, h, j: (i, 0, h)),
                  pl.BlockSpec((1, s, LANES), lambda i, h, j: (i, 0, h)),
                  pl.BlockSpec((tq, tq), lambda i, h, j: (0, 0)),
                  pl.BlockSpec((1, LANES), lambda i, h, j: (0, h))],
        out_specs=pl.BlockSpec((1, tq, LANES), lambda i, h, j: (i, j, h)),
        out_shape=jax.ShapeDtypeStruct((b, s, w), BF16),
        compiler_params=pltpu.CompilerParams(
            dimension_semantics=("arbitrary", "arbitrary", "arbitrary"),
            vmem_limit_bytes=VMEM_LIMIT),
        name="stick_breaking",
    )(q, k, v, tri, g_sb)


def _mid_kernel(x_ref, yc_ref, ys_ref, wo1_ref, wo2_ref, gx_ref, wq_ref, kv_ref, wom_ref,
                gm_ref, wr_ref, br_ref, ltri_ref,
                x2_ref, h3_ref, route_ref, cnt_ref, carry_ref):
    tm = x_ref.shape[1]
    d = x_ref.shape[2]
    hd = d // X_HEADS

    @pl.when((pl.program_id(0) == 0) & (pl.program_id(1) == 0))
    def _():
        carry_ref[...] = jnp.zeros_like(carry_ref)

    x1 = (x_ref[0]
          + jnp.dot(yc_ref[0], wo1_ref[...], preferred_element_type=F32)
          + jnp.dot(ys_ref[0], wo2_ref[...], preferred_element_type=F32))
    h2 = _rms(x1, gx_ref[...]).astype(BF16)
    qm = (jnp.dot(h2, wq_ref[...], preferred_element_type=F32) * (hd ** -0.5)).astype(BF16)
    heads = []
    for h in range(X_HEADS):
        kh = kv_ref[0, :, h * hd:(h + 1) * hd]
        vh = kv_ref[0, :, d + h * hd:d + (h + 1) * hd]
        sc = lax.dot_general(qm[:, h * hd:(h + 1) * hd], kh, (((1,), (1,)), ((), ())),
                             preferred_element_type=F32)
        e = jnp.exp(sc - jnp.max(sc, axis=-1, keepdims=True))
        p = e / jnp.sum(e, axis=-1, keepdims=True)
        heads.append(jnp.dot(p.astype(BF16), vh, preferred_element_type=F32).astype(BF16))
    x2 = x1 + jnp.dot(jnp.concatenate(heads, axis=-1), wom_ref[...], preferred_element_type=F32)
    x2_ref[0] = x2
    h3 = _rms(x2, gm_ref[...])
    _store_row_tiles(h3_ref, h3)

    logits = jnp.dot(h3, wr_ref[...], preferred_element_type=F32,
                     precision=lax.Precision.HIGHEST) + br_ref[...]
    lane = lax.broadcasted_iota(jnp.int32, (tm, N_EXPERTS), 1).astype(F32)
    vals, idxs = [], []
    l = logits
    for _ in range(TOP_K):
        m = jnp.max(l, axis=-1, keepdims=True)
        idx = jnp.min(jnp.where(l == m, lane, float(N_EXPERTS)), axis=-1, keepdims=True)
        vals.append(m)
        idxs.append(idx)
        l = jnp.where(lane == idx, -jnp.inf, l)
    es = [jnp.exp(vk - vals[0]) for vk in vals]
    den = es[0] + es[1] + es[2] + es[3]
    gates = [ek / den for ek in es]

    sel = jnp.zeros((tm, N_EXPERTS), F32)
    for idx in idxs:
        sel = sel + (lane == idx).astype(F32)
    before = jnp.dot(ltri_ref[...], sel.astype(BF16), preferred_element_type=F32) + carry_ref[...]
    ranks = [jnp.sum(jnp.where(lane == idx, before, 0.0), axis=-1, keepdims=True) for idx in idxs]
    carry_ref[...] += jnp.sum(sel, axis=0, keepdims=True)
    cnt_ref[...] = carry_ref[...]

    lane_o = lax.broadcasted_iota(jnp.int32, (tm, LANES), 1)
    r = jnp.zeros((tm, LANES), F32)
    for k in range(TOP_K):
        r = jnp.where(lane_o == k, idxs[k], r)
        r = jnp.where(lane_o == TOP_K + k, gates[k], r)
        r = jnp.where(lane_o == 2 * TOP_K + k, ranks[k], r)
    route_ref[0] = r


def _mid(x, yc, ys, wo1, wo2, g_x, w_q, kv, w_om, g_moe, w_r, b_r, ltri):
    b, s, d = x.shape
    tm = ROW_TILE
    m = kv.shape[1]
    row = lambda i, j: (i, j, 0)
    fixed = lambda i, j: (0, 0)
    return pl.pallas_call(
        _mid_kernel,
        grid=(b, s // tm),
        in_specs=[pl.BlockSpec((1, tm, d), row),
                  pl.BlockSpec((1, tm, CONV_DIM), row),
                  pl.BlockSpec((1, tm, SB_DIM), row),
                  pl.BlockSpec((CONV_DIM, d), fixed),
                  pl.BlockSpec((SB_DIM, d), fixed),
                  pl.BlockSpec((1, d), fixed),
                  pl.BlockSpec((d, d), fixed),
                  pl.BlockSpec((1, m, 2 * d), lambda i, j: (i, 0, 0)),
                  pl.BlockSpec((d, d), fixed),
                  pl.BlockSpec((1, d), fixed),
                  pl.BlockSpec((d, N_EXPERTS), fixed),
                  pl.BlockSpec((1, N_EXPERTS), fixed),
                  pl.BlockSpec((tm, tm), fixed)],
        out_specs=[pl.BlockSpec((1, tm, d), row),
                   pl.BlockSpec((tm, d // LANES, LANES), lambda i, j: (i * (s // tm) + j, 0, 0)),
                   pl.BlockSpec((1, tm, LANES), row),
                   pl.BlockSpec((1, N_EXPERTS), fixed)],
        out_shape=[jax.ShapeDtypeStruct((b, s, d), F32),
                   jax.ShapeDtypeStruct((b * s, d // LANES, LANES), F32),
                   jax.ShapeDtypeStruct((b, s, LANES), F32),
                   jax.ShapeDtypeStruct((1, N_EXPERTS), F32)],
        scratch_shapes=[pltpu.VMEM((1, N_EXPERTS), F32)],
        compiler_params=pltpu.CompilerParams(
            dimension_semantics=("arbitrary", "arbitrary"), vmem_limit_bytes=VMEM_LIMIT),
        name="outproj_xattn_router",
    )(x, yc, ys, wo1, wo2, g_x, w_q, kv, w_om, g_moe, w_r, b_r, ltri)


def _row_copy(src, src_row, dst, dst_row, sem):
    return pltpu.make_async_copy(src.at[src_row], dst.at[dst_row], sem)


def _dispatch_kernel(cnt_ref, pad_ref, start_ref, dest_ref, h_ref, zero_ref, xs_ref, sem, zsem):
    i = pl.program_id(0)
    n_assign = dest_ref.shape[0]
    tm = n_assign // TOP_K
    te = zero_ref.shape[0]
    base = i * tm

    def issue(t, c):
        for k in range(TOP_K):
            _row_copy(h_ref, base + t, xs_ref, dest_ref[t * TOP_K + k], sem).start()
        return c

    lax.fori_loop(0, tm, issue, 0)

    @pl.when(i == 0)
    def _():
        def per_expert(e, c):
            first = start_ref[e] + cnt_ref[e]
            n_pad = pad_ref[e] - cnt_ref[e]

            def pad_start(r, cc):
                _row_copy(zero_ref, 0, xs_ref, first + r, zsem).start()
                return cc

            def pad_wait(r, cc):
                _row_copy(zero_ref, 0, xs_ref, first + r, zsem).wait()
                return cc

            lax.fori_loop(0, n_pad, pad_start, 0)
            lax.fori_loop(0, n_pad, pad_wait, 0)
            return c

        lax.fori_loop(0, N_EXPERTS, per_expert, 0)
        last = N_EXPERTS - 1
        n_used = (start_ref[last] + pad_ref[last]) // te

        def tail_copy(j):
            rows = pl.ds(pl.multiple_of(j * te, te), te)
            return pltpu.make_async_copy(zero_ref, xs_ref.at[rows], zsem)

        def tail_start(j, c):
            tail_copy(j).start()
            return c

        def tail_wait(j, c):
            tail_copy(j).wait()
            return c

        lax.fori_loop(n_used, xs_ref.shape[0] // te, tail_start, 0)
        lax.fori_loop(n_used, xs_ref.shape[0] // te, tail_wait, 0)

    def drain(t, c):
        for k in range(TOP_K):
            _row_copy(h_ref, base + t, xs_ref, dest_ref[t * TOP_K + k], sem).wait()
        return c

    lax.fori_loop(0, tm, drain, 0)


def _dispatch(counts, padded, start, dest_flat, h3, n_rows):
    n = h3.shape[0]
    tm = ROW_TILE
    zero = jnp.zeros((EXPERT_TILE,) + h3.shape[1:], F32)
    return pl.pallas_call(
        _dispatch_kernel,
        grid_spec=pltpu.PrefetchScalarGridSpec(
            num_scalar_prefetch=3,
            grid=(n // tm,),
            in_specs=[pl.BlockSpec((tm * TOP_K,), lambda i, *_: (i,), memory_space=pltpu.SMEM),
                      pl.BlockSpec(memory_space=pl.ANY),
                      pl.BlockSpec(memory_space=pl.ANY)],
            out_specs=pl.BlockSpec(memory_space=pl.ANY),
            scratch_shapes=[pltpu.SemaphoreType.DMA, pltpu.SemaphoreType.DMA]),
        out_shape=jax.ShapeDtypeStruct((n_rows,) + h3.shape[1:], F32),
        compiler_params=pltpu.CompilerParams(dimension_semantics=("arbitrary",)),
        name="moe_dispatch",
    )(counts, padded, start, dest_flat, h3, zero)


def _expert_kernel(be_ref, nu_ref, xs_ref, w1_ref, b1_ref, w2_ref, b2_ref, o_ref, w1c_ref, w2c_ref):
    i = pl.program_id(0)
    de = w2_ref.shape[1]

    @pl.when(i >= nu_ref[0])
    def _():
        o_ref[...] = jnp.zeros_like(o_ref)

    @pl.when(i < nu_ref[0])
    def _():
        changed = (i == 0) | (be_ref[i] != be_ref[jnp.maximum(i - 1, 0)])

        @pl.when(changed)
        def _():
            w1c_ref[...] = w1_ref[0].astype(BF16)
            w2c_ref[...] = w2_ref[0].astype(BF16)

        xb = _load_row_tiles(xs_ref).astype(BF16)
        hid = jnp.dot(xb, w1c_ref[...], preferred_element_type=F32) + b1_ref[0]
        glu = jnp.minimum(hid[:, :de], SWIGLU_LIMIT)
        lin = jnp.clip(hid[:, de:], -SWIGLU_LIMIT, SWIGLU_LIMIT)
        act = glu * (1.0 / (1.0 + jnp.exp(-SWIGLU_ALPHA * glu))) * (lin + 1.0)
        out = jnp.dot(act.astype(BF16), w2c_ref[...], preferred_element_type=F32) + b2_ref[0]
        _store_row_tiles(o_ref, out)


def _experts(block_e, n_used, xs, w1, b1, w2, b2):
    n_rows, nt, _ = xs.shape
    tm = EXPERT_TILE
    ne, d, dh = w1.shape
    de = w2.shape[1]
    rows = lambda i, be, nu: (jnp.maximum(jnp.minimum(i, nu[0] - 1), 0), 0, 0)
    per_e = lambda i, be, nu: (be[i], 0, 0)
    return pl.pallas_call(
        _expert_kernel,
        grid_spec=pltpu.PrefetchScalarGridSpec(
            num_scalar_prefetch=2,
            grid=(n_rows // tm,),
            in_specs=[pl.BlockSpec((tm, nt, LANES), rows),
                      pl.BlockSpec((1, d, dh), per_e),
                      pl.BlockSpec((1, 1, dh), per_e),
                      pl.BlockSpec((1, de, d), per_e),
                      pl.BlockSpec((1, 1, d), per_e)],
            out_specs=pl.BlockSpec((tm, nt, LANES), lambda i, be, nu: (i, 0, 0)),
            scratch_shapes=[pltpu.VMEM((d, dh), BF16), pltpu.VMEM((de, d), BF16)]),
        out_shape=jax.ShapeDtypeStruct(xs.shape, F32),
        compiler_params=pltpu.CompilerParams(
            dimension_semantics=("arbitrary",), vmem_limit_bytes=VMEM_LIMIT),
        name="moe_experts",
    )(block_e, n_used, xs, w1, b1.reshape(ne, 1, dh), w2, b2.reshape(ne, 1, d))


def _combine_kernel(dest_ref, x2_ref, route_ref, g_ref, ys_ref, o_ref, buf_ref, sem):
    tm = x2_ref.shape[0]

    def issue(t, c):
        for k in range(TOP_K):
            pltpu.make_async_copy(ys_ref.at[dest_ref[t * TOP_K + k]], buf_ref.at[k, t], sem).start()
        return c

    def drain(t, c):
        for k in range(TOP_K):
            pltpu.make_async_copy(ys_ref.at[dest_ref[t * TOP_K + k]], buf_ref.at[k, t], sem).wait()
        return c

    lax.fori_loop(0, tm, issue, 0)
    lax.fori_loop(0, tm, drain, 0)
    route = route_ref[...]
    x3 = x2_ref[...]
    for k in range(TOP_K):
        x3 = x3 + _load_row_tiles(buf_ref.at[k]) * route[:, TOP_K + k:TOP_K + k + 1]
    o_ref[...] = _rms(x3, g_ref[...])


def _combine(dest_flat, x2, route, g_final, ys):
    n, d = x2.shape
    tm = ROW_TILE
    return pl.pallas_call(
        _combine_kernel,
        grid=(n // tm,),
        in_specs=[pl.BlockSpec((tm * TOP_K,), lambda i: (i,), memory_space=pltpu.SMEM),
                  pl.BlockSpec((tm, d), lambda i: (i, 0)),
                  pl.BlockSpec((tm, LANES), lambda i: (i, 0)),
                  pl.BlockSpec((1, d), lambda i: (0, 0)),
                  pl.BlockSpec(memory_space=pl.ANY)],
        out_specs=pl.BlockSpec((tm, d), lambda i: (i, 0)),
        out_shape=jax.ShapeDtypeStruct((n, d), F32),
        scratch_shapes=[pltpu.VMEM((TOP_K, tm, d // LANES, LANES), F32), pltpu.SemaphoreType.DMA],
        compiler_params=pltpu.CompilerParams(
            dimension_semantics=("arbitrary",), vmem_limit_bytes=VMEM_LIMIT),
        name="moe_combine_norm",
    )(dest_flat, x2, route, g_final, ys)


def _layer(x, mem, g_mix, w_in, conv_w, g_conv_out, g_sb_out, w_out, g_xattn, g_mem,
           w_q_mem, w_kv_mem, w_o_mem, g_moe, w_router, b_router, w1, b1, w2, b2):
    b, s, d = x.shape
    n = b * s
    row2 = lambda g: g.reshape(1, -1)

    grp = jnp.arange(CONV_DIM) // CONV_GROUP_DIM
    gmat = jnp.where(grp[:, None] == grp[None, :], 1.0 / CONV_GROUP_DIM, 0.0).astype(BF16)
    ar = jnp.arange(SB_TILE)
    tri = (ar[:, None] > ar[None, :]).astype(BF16)
    ar = jnp.arange(ROW_TILE)
    ltri = (ar[:, None] > ar[None, :]).astype(BF16)

    kv = _mem_kv(mem, row2(g_mem), w_kv_mem.astype(BF16))
    yc, q, k, v = _inproj(x, row2(g_mix), w_in.astype(BF16), conv_w, row2(g_conv_out), gmat)
    ys = _stick_breaking(q, k, v, tri, row2(g_sb_out))
    w_out_b = w_out.astype(BF16)
    x2, h3, route, counts = _mid(x, yc, ys, w_out_b[:CONV_DIM], w_out_b[CONV_DIM:], row2(g_xattn),
                                 w_q_mem.astype(BF16), kv, w_o_mem.astype(BF16), row2(g_moe),
                                 w_router, row2(b_router), ltri)

    te = EXPERT_TILE
    route = route.reshape(n, LANES)
    counts = counts.reshape(N_EXPERTS).astype(jnp.int32)
    padded = (counts + te - 1) // te * te
    padded_end = jnp.cumsum(padded)
    start = padded_end - padded
    top_i = route[:, 0:TOP_K].astype(jnp.int32)
    rank = route[:, 2 * TOP_K:3 * TOP_K].astype(jnp.int32)
    dest = (start[top_i] + rank).reshape(n * TOP_K)
    n_rows = n * TOP_K + N_EXPERTS * te
    n_blocks = n_rows // te
    n_used = (padded_end[-1] // te).astype(jnp.int32)
    blk = jnp.minimum(jnp.arange(n_blocks, dtype=jnp.int32), n_used - 1) * te
    block_e = jnp.minimum(jnp.searchsorted(padded_end, blk, side='right'),
                          N_EXPERTS - 1).astype(jnp.int32)

    xs = _dispatch(counts, padded, start, dest, h3, n_rows)
    yo = _experts(block_e, n_used.reshape(1), xs, w1, b1, w2, b2)
    return dest, x2.reshape(n, d), route, yo


def kernel(x, mem, g_mix, w_in, conv_w, g_conv_out, g_sb_out, w_out, g_xattn, g_mem, w_q_mem,
           w_kv_mem, w_o_mem, g_moe, w_router, b_router, w1, b1, w2, b2, g_final):
    b, s, d = x.shape
    assert w_in.shape[0] == 1, "single-layer stack"
    dest, x2, route, yo = _layer(x, mem, g_mix[0], w_in[0], conv_w[0], g_conv_out[0], g_sb_out[0],
                                 w_out[0], g_xattn[0], g_mem[0], w_q_mem[0], w_kv_mem[0],
                                 w_o_mem[0], g_moe[0], w_router[0], b_router[0],
                                 w1[0], b1[0], w2[0], b2[0])
    out = _combine(dest, x2, route, g_final.reshape(1, d), yo)
    return out.reshape(b, s, d)
```

```python
import jax
import jax.numpy as jnp
from jax import lax
from jax.experimental import pallas as pl
from jax.experimental.pallas import tpu as pltpu

F32 = jnp.float32
BF16 = jnp.bfloat16

EPS = 1e-5
CONV_DIM = 512
SB_DIM = 512
SB_HEAD_DIM = 64
CONV_GROUP_DIM = 64
X_HEADS = 4
N_EXPERTS = 32
TOP_K = 4
SWIGLU_LIMIT = 7.0
SWIGLU_ALPHA = 1.702

LANES = 128
ROW_TILE = 256
SB_TILE = 256
EXPERT_TILE = 256
VMEM_LIMIT = 56 * 1024 * 1024


def _rms(x, g):
    return x * lax.rsqrt(jnp.mean(x * x, axis=-1, keepdims=True) + EPS) * g


def _store_row_tiles(ref, val):
    for j in range(ref.shape[-2]):
        ref[:, j, :] = val[:, j * LANES:(j + 1) * LANES]


def _load_row_tiles(ref):
    return jnp.concatenate([ref[:, j, :] for j in range(ref.shape[-2])], axis=-1)


def _split_dot(a, m):
    hi = a.astype(BF16)
    lo = (a - hi.astype(F32)).astype(BF16)
    return (jnp.dot(hi, m, preferred_element_type=F32)
            + jnp.dot(lo, m, preferred_element_type=F32))


def _mem_kv_kernel(mem_ref, g_ref, w_ref, kv_ref):
    h = _rms(mem_ref[0], g_ref[...]).astype(BF16)
    kv_ref[0] = jnp.dot(h, w_ref[...], preferred_element_type=F32).astype(BF16)


def _mem_kv(mem, g_mem, w_kv):
    b, m, d = mem.shape
    return pl.pallas_call(
        _mem_kv_kernel,
        grid=(b,),
        in_specs=[pl.BlockSpec((1, m, d), lambda i: (i, 0, 0)),
                  pl.BlockSpec((1, d), lambda i: (0, 0)),
                  pl.BlockSpec((d, 2 * d), lambda i: (0, 0))],
        out_specs=pl.BlockSpec((1, m, 2 * d), lambda i: (i, 0, 0)),
        out_shape=jax.ShapeDtypeStruct((b, m, 2 * d), BF16),
        compiler_params=pltpu.CompilerParams(
            dimension_semantics=("arbitrary",), vmem_limit_bytes=VMEM_LIMIT),
        name="mem_kv",
    )(mem, g_mem, w_kv)


def _inproj_kernel(x_ref, g_ref, w_ref, cw_ref, gc_ref, gmat_ref,
                   yc_ref, q_ref, k_ref, v_ref, prev_ref):
    tm = x_ref.shape[1]
    h = _rms(x_ref[0], g_ref[...]).astype(BF16)
    proj = jnp.dot(h, w_ref[...], preferred_element_type=F32)
    c = CONV_DIM
    u = proj[:, c:2 * c] * proj[:, 2 * c:3 * c]

    @pl.when(pl.program_id(1) == 0)
    def _():
        prev_ref[...] = jnp.zeros_like(prev_ref)

    rows = lax.broadcasted_iota(jnp.int32, (tm, 1), 0)
    p1 = prev_ref[7:8, :]
    p2 = prev_ref[6:7, :]
    u1 = jnp.where(rows == 0, p1, pltpu.roll(u, 1, axis=0))
    u2 = jnp.where(rows == 0, p2, jnp.where(rows == 1, p1, pltpu.roll(u, 2, axis=0)))
    prev_ref[...] = u[tm - 8:tm, :]
    y = proj[:, 0:c] * (cw_ref[0:1, :] * u2 + cw_ref[1:2, :] * u1 + cw_ref[2:3, :] * u)
    ms = _split_dot(y * y, gmat_ref[...])
    yc_ref[0] = (y * lax.rsqrt(ms + EPS) * gc_ref[...]).astype(BF16)
    o = 3 * c
    q_ref[0] = (proj[:, o:o + SB_DIM] * (SB_HEAD_DIM ** -0.5)).astype(BF16)
    k_ref[0] = proj[:, o + SB_DIM:o + 2 * SB_DIM].astype(BF16)
    v_ref[0] = proj[:, o + 2 * SB_DIM:o + 3 * SB_DIM].astype(BF16)


def _inproj(x, g_mix, w_in, conv_w, g_conv, gmat):
    b, s, d = x.shape
    tm = ROW_TILE
    p = w_in.shape[1]
    row = lambda i, j: (i, j, 0)
    fixed = lambda i, j: (0, 0)
    outs = [jax.ShapeDtypeStruct((b, s, CONV_DIM), BF16)] * 4
    return pl.pallas_call(
        _inproj_kernel,
        grid=(b, s // tm),
        in_specs=[pl.BlockSpec((1, tm, d), row),
                  pl.BlockSpec((1, d), fixed),
                  pl.BlockSpec((d, p), fixed),
                  pl.BlockSpec(conv_w.shape, fixed),
                  pl.BlockSpec((1, CONV_DIM), fixed),
                  pl.BlockSpec((CONV_DIM, CONV_DIM), fixed)],
        out_specs=[pl.BlockSpec((1, tm, CONV_DIM), row)] * 4,
        out_shape=outs,
        scratch_shapes=[pltpu.VMEM((8, CONV_DIM), F32)],
        compiler_params=pltpu.CompilerParams(
            dimension_semantics=("arbitrary", "arbitrary"), vmem_limit_bytes=VMEM_LIMIT),
        name="inproj_conv",
    )(x, g_mix, w_in, conv_w, g_conv, gmat)


def _sb_kernel(q_ref, k_ref, v_ref, tri_ref, g_ref, o_ref):
    tq = q_ref.shape[1]
    tk = tq
    qi = pl.program_id(2)
    q = q_ref[0]
    lane = lax.broadcasted_iota(jnp.int32, (1, LANES), 1)
    in_a = lane < SB_HEAD_DIM
    zero = jnp.zeros_like(q)
    q_heads = (jnp.where(in_a, q, zero), jnp.where(in_a, zero, q))
    tri = tri_ref[...]

    def head_tile(qh, kb, vb, run, acc, valid):
        z = lax.dot_general(qh, kb, (((1,), (1,)), ((), ())), preferred_element_type=F32)
        sp = jnp.maximum(z, 0.0) + jnp.log(1.0 + jnp.exp(-jnp.abs(z)))
        lk = -sp
        if valid is not None:
            lk = jnp.where(valid, lk, 0.0)
        rest = _split_dot(lk, tri)
        tot = rest[:, 0:1] + lk[:, 0:1]
        a = jnp.exp(z - sp + rest + run)
        if valid is not None:
            a = jnp.where(valid, a, 0.0)
        acc = acc + jnp.dot(a.astype(BF16), vb, preferred_element_type=F32)
        return run + tot, acc

    def tile(j, carry, valid):
        start = pl.multiple_of(j * tk, tk)
        kb = k_ref[0, pl.ds(start, tk), :]
        vb = v_ref[0, pl.ds(start, tk), :]
        ra, aa = head_tile(q_heads[0], kb, vb, carry[0], carry[1], valid)
        rb, ab = head_tile(q_heads[1], kb, vb, carry[2], carry[3], valid)
        return ra, aa, rb, ab

    r0 = jnp.zeros((tq, 1), F32)
    a0 = jnp.zeros((tq, LANES), F32)
    t_idx = lax.broadcasted_iota(jnp.int32, (tq, tk), 0)
    s_idx = lax.broadcasted_iota(jnp.int32, (tq, tk), 1)
    carry = tile(qi, (r0, a0, r0, a0), s_idx < t_idx)
    carry = lax.fori_loop(0, qi, lambda n, c: tile(qi - 1 - n, c, None), carry)
    out = jnp.where(in_a, carry[1], carry[3])
    sq = out * out
    ss_a = jnp.sum(jnp.where(in_a, sq, 0.0), axis=-1, keepdims=True)
    ss_b = jnp.sum(jnp.where(in_a, 0.0, sq), axis=-1, keepdims=True)
    ms = jnp.where(in_a, ss_a, ss_b) * (1.0 / SB_HEAD_DIM)
    o_ref[0] = (out * lax.rsqrt(ms + EPS) * g_ref[...]).astype(BF16)


def _stick_breaking(q, k, v, tri, g_sb):
    b, s, w = q.shape
    tq = SB_TILE
    return pl.pallas_call(
        _sb_kernel,
        grid=(b, w // LANES, s // tq),
        in_specs=[pl.BlockSpec((1, tq, LANES), lambda i, h, j: (i, j, h)),
                  pl.BlockSpec((1, s, LANES), lambda i, h, j: (i, 0, h)),
                  pl.BlockSpec((1, s, LANES), lambda i, h, j: (i, 0, h)),
                  pl.BlockSpec((tq, tq), lambda i, h, j: (0, 0)),
                  pl.BlockSpec((1, LANES), lambda i, h, j: (0, h))],
        out_specs=pl.BlockSpec((1, tq, LANES), lambda i, h, j: (i, j, h)),
        out_shape=jax.ShapeDtypeStruct((b, s, w), BF16),
        compiler_params=pltpu.CompilerParams(
            dimension_semantics=("arbitrary", "arbitrary", "arbitrary"),
            vmem_limit_bytes=VMEM_LIMIT),
        name="stick_breaking",
    )(q, k, v, tri, g_sb)


def _mid_kernel(x_ref, yc_ref, ys_ref, wo1_ref, wo2_ref, gx_ref, wq_ref, kv_ref, wom_ref,
                gm_ref, wr_ref, br_ref, ltri_ref,
                x2_ref, h3_ref, route_ref, cnt_ref, carry_ref):
    tm = x_ref.shape[1]
    d = x_ref.shape[2]
    hd = d // X_HEADS

    @pl.when((pl.program_id(0) == 0) & (pl.program_id(1) == 0))
    def _():
        carry_ref[...] = jnp.zeros_like(carry_ref)

    x1 = (x_ref[0]
          + jnp.dot(yc_ref[0], wo1_ref[...], preferred_element_type=F32)
          + jnp.dot(ys_ref[0], wo2_ref[...], preferred_element_type=F32))
    h2 = _rms(x1, gx_ref[...]).astype(BF16)
    qm = (jnp.dot(h2, wq_ref[...], preferred_element_type=F32) * (hd ** -0.5)).astype(BF16)
    heads = []
    for h in range(X_HEADS):
        kh = kv_ref[0, :, h * hd:(h + 1) * hd]
        vh = kv_ref[0, :, d + h * hd:d + (h + 1) * hd]
        sc = lax.dot_general(qm[:, h * hd:(h + 1) * hd], kh, (((1,), (1,)), ((), ())),
                             preferred_element_type=F32)
        e = jnp.exp(sc - jnp.max(sc, axis=-1, keepdims=True))
        p = e / jnp.sum(e, axis=-1, keepdims=True)
        heads.append(jnp.dot(p.astype(BF16), vh, preferred_element_type=F32).astype(BF16))
    x2 = x1 + jnp.dot(jnp.concatenate(heads, axis=-1), wom_ref[...], preferred_element_type=F32)
    x2_ref[0] = x2
    h3 = _rms(x2, gm_ref[...])
    _store_row_tiles(h3_ref, h3)

    logits = jnp.dot(h3, wr_ref[...], preferred_element_type=F32,
                     precision=lax.Precision.HIGHEST) + br_ref[...]
    lane = lax.broadcasted_iota(jnp.int32, (tm, N_EXPERTS), 1).astype(F32)
    vals, idxs = [], []
    l = logits
    for _ in range(TOP_K):
        m = jnp.max(l, axis=-1, keepdims=True)
        idx = jnp.min(jnp.where(l == m, lane, float(N_EXPERTS)), axis=-1, keepdims=True)
        vals.append(m)
        idxs.append(idx)
        l = jnp.where(lane == idx, -jnp.inf, l)
    es = [jnp.exp(vk - vals[0]) for vk in vals]
    den = es[0] + es[1] + es[2] + es[3]
    gates = [ek / den for ek in es]

    sel = jnp.zeros((tm, N_EXPERTS), F32)
    for idx in idxs:
        sel = sel + (lane == idx).astype(F32)
    before = jnp.dot(ltri_ref[...], sel.astype(BF16), preferred_element_type=F32) + carry_ref[...]
    ranks = [jnp.sum(jnp.where(lane == idx, before, 0.0), axis=-1, keepdims=True) for idx in idxs]
    carry_ref[...] += jnp.sum(sel, axis=0, keepdims=True)
    cnt_ref[...] = carry_ref[...]

    lane_o = lax.broadcasted_iota(jnp.int32, (tm, LANES), 1)
    r = jnp.zeros((tm, LANES), F32)
    for k in range(TOP_K):
        r = jnp.where(lane_o == k, idxs[k], r)
        r = jnp.where(lane_o == TOP_K + k, gates[k], r)
        r = jnp.where(lane_o == 2 * TOP_K + k, ranks[k], r)
    route_ref[0] = r


def _mid(x, yc, ys, wo1, wo2, g_x, w_q, kv, w_om, g_moe, w_r, b_r, ltri):
    b, s, d = x.shape
    tm = ROW_TILE
    m = kv.shape[1]
    row = lambda i, j: (i, j, 0)
    fixed = lambda i, j: (0, 0)
    return pl.pallas_call(
        _mid_kernel,
        grid=(b, s // tm),
        in_specs=[pl.BlockSpec((1, tm, d), row),
                  pl.BlockSpec((1, tm, CONV_DIM), row),
                  pl.BlockSpec((1, tm, SB_DIM), row),
                  pl.BlockSpec((CONV_DIM, d), fixed),
                  pl.BlockSpec((SB_DIM, d), fixed),
                  pl.BlockSpec((1, d), fixed),
                  pl.BlockSpec((d, d), fixed),
                  pl.BlockSpec((1, m, 2 * d), lambda i, j: (i, 0, 0)),
                  pl.BlockSpec((d, d), fixed),
                  pl.BlockSpec((1, d), fixed),
                  pl.BlockSpec((d, N_EXPERTS), fixed),
                  pl.BlockSpec((1, N_EXPERTS), fixed),
                  pl.BlockSpec((tm, tm), fixed)],
        out_specs=[pl.BlockSpec((1, tm, d), row),
                   pl.BlockSpec((tm, d // LANES, LANES), lambda i, j: (i * (s // tm) + j, 0, 0)),
                   pl.BlockSpec((1, tm, LANES), row),
                   pl.BlockSpec((1, N_EXPERTS), fixed)],
        out_shape=[jax.ShapeDtypeStruct((b, s, d), F32),
                   jax.ShapeDtypeStruct((b * s, d // LANES, LANES), F32),
                   jax.ShapeDtypeStruct((b, s, LANES), F32),
                   jax.ShapeDtypeStruct((1, N_EXPERTS), F32)],
        scratch_shapes=[pltpu.VMEM((1, N_EXPERTS), F32)],
        compiler_params=pltpu.CompilerParams(
            dimension_semantics=("arbitrary", "arbitrary"), vmem_limit_bytes=VMEM_LIMIT),
        name="outproj_xattn_router",
    )(x, yc, ys, wo1, wo2, g_x, w_q, kv, w_om, g_moe, w_r, b_r, ltri)


def _row_copy(src, src_row, dst, dst_row, sem):
    return pltpu.make_async_copy(src.at[src_row], dst.at[dst_row], sem)


def _dispatch_kernel(cnt_ref, pad_ref, start_ref, dest_ref, h_ref, zero_ref, xs_ref, sem, zsem):
    i = pl.program_id(0)
    n_assign = dest_ref.shape[0]
    tm = n_assign // TOP_K
    te = zero_ref.shape[0]

    def issue(t, c):
        for k in range(TOP_K):
            _row_copy(h_ref, t, xs_ref, dest_ref[t * TOP_K + k], sem).start()
        return c

    lax.fori_loop(0, tm, issue, 0)

    @pl.when(i == 0)
    def _():
        def per_expert(e, c):
            first = start_ref[e] + cnt_ref[e]
            n_pad = pad_ref[e] - cnt_ref[e]

            def pad_start(r, cc):
                _row_copy(zero_ref, 0, xs_ref, first + r, zsem).start()
                return cc

            def pad_wait(r, cc):
                _row_copy(zero_ref, 0, xs_ref, first + r, zsem).wait()
                return cc

            lax.fori_loop(0, n_pad, pad_start, 0)
            lax.fori_loop(0, n_pad, pad_wait, 0)
            return c

        lax.fori_loop(0, N_EXPERTS, per_expert, 0)
        last = N_EXPERTS - 1
        n_used = (start_ref[last] + pad_ref[last]) // te

        def tail_copy(j):
            rows = pl.ds(pl.multiple_of(j * te, te), te)
            return pltpu.make_async_copy(zero_ref, xs_ref.at[rows], zsem)

        def tail_start(j, c):
            tail_copy(j).start()
            return c

        def tail_wait(j, c):
            tail_copy(j).wait()
            return c

        lax.fori_loop(n_used, xs_ref.shape[0] // te, tail_start, 0)
        lax.fori_loop(n_used, xs_ref.shape[0] // te, tail_wait, 0)

    def drain(t, c):
        for k in range(TOP_K):
            _row_copy(h_ref, t, xs_ref, dest_ref[t * TOP_K + k], sem).wait()
        return c

    lax.fori_loop(0, tm, drain, 0)


def _dispatch(counts, padded, start, dest_flat, h3, n_rows):
    n = h3.shape[0]
    tm = ROW_TILE
    zero = jnp.zeros((EXPERT_TILE,) + h3.shape[1:], F32)
    return pl.pallas_call(
        _dispatch_kernel,
        grid_spec=pltpu.PrefetchScalarGridSpec(
            num_scalar_prefetch=3,
            grid=(n // tm,),
            in_specs=[pl.BlockSpec((tm * TOP_K,), lambda i, *_: (i,), memory_space=pltpu.SMEM),
                      pl.BlockSpec((tm,) + h3.shape[1:], lambda i, *_: (i, 0, 0)),
                      pl.BlockSpec(zero.shape, lambda i, *_: (0, 0, 0))],
            out_specs=pl.BlockSpec(memory_space=pl.ANY),
            scratch_shapes=[pltpu.SemaphoreType.DMA, pltpu.SemaphoreType.DMA]),
        out_shape=jax.ShapeDtypeStruct((n_rows,) + h3.shape[1:], F32),
        compiler_params=pltpu.CompilerParams(
            dimension_semantics=("arbitrary",), vmem_limit_bytes=VMEM_LIMIT),
        name="moe_dispatch",
    )(counts, padded, start, dest_flat, h3, zero)


def _expert_kernel(be_ref, nu_ref, xs_ref, w1_ref, b1_ref, w2_ref, b2_ref, o_ref, w1c_ref, w2c_ref):
    i = pl.program_id(0)
    de = w2_ref.shape[1]

    @pl.when(i >= nu_ref[0])
    def _():
        o_ref[...] = jnp.zeros_like(o_ref)

    @pl.when(i < nu_ref[0])
    def _():
        changed = (i == 0) | (be_ref[i] != be_ref[jnp.maximum(i - 1, 0)])

        @pl.when(changed)
        def _():
            w1c_ref[...] = w1_ref[0].astype(BF16)
            w2c_ref[...] = w2_ref[0].astype(BF16)

        xb = _load_row_tiles(xs_ref).astype(BF16)
        hid = jnp.dot(xb, w1c_ref[...], preferred_element_type=F32) + b1_ref[0]
        glu = jnp.minimum(hid[:, :de], SWIGLU_LIMIT)
        lin = jnp.clip(hid[:, de:], -SWIGLU_LIMIT, SWIGLU_LIMIT)
        act = glu * (1.0 / (1.0 + jnp.exp(-SWIGLU_ALPHA * glu))) * (lin + 1.0)
        out = jnp.dot(act.astype(BF16), w2c_ref[...], preferred_element_type=F32) + b2_ref[0]
        _store_row_tiles(o_ref, out)


def _experts(block_e, n_used, xs, w1, b1, w2, b2):
    n_rows, nt, _ = xs.shape
    tm = EXPERT_TILE
    ne, d, dh = w1.shape
    de = w2.shape[1]
    rows = lambda i, be, nu: (jnp.maximum(jnp.minimum(i, nu[0] - 1), 0), 0, 0)
    per_e = lambda i, be, nu: (be[i], 0, 0)
    return pl.pallas_call(
        _expert_kernel,
        grid_spec=pltpu.PrefetchScalarGridSpec(
            num_scalar_prefetch=2,
            grid=(n_rows // tm,),
            in_specs=[pl.BlockSpec((tm, nt, LANES), rows),
                      pl.BlockSpec((1, d, dh), per_e),
                      pl.BlockSpec((1, 1, dh), per_e),
                      pl.BlockSpec((1, de, d), per_e),
                      pl.BlockSpec((1, 1, d), per_e)],
            out_specs=pl.BlockSpec((tm, nt, LANES), lambda i, be, nu: (i, 0, 0)),
            scratch_shapes=[pltpu.VMEM((d, dh), BF16), pltpu.VMEM((de, d), BF16)]),
        out_shape=jax.ShapeDtypeStruct(xs.shape, F32),
        compiler_params=pltpu.CompilerParams(
            dimension_semantics=("arbitrary",), vmem_limit_bytes=VMEM_LIMIT),
        name="moe_experts",
    )(block_e, n_used, xs, w1, b1.reshape(ne, 1, dh), w2, b2.reshape(ne, 1, d))


def _combine_kernel(dest_ref, x2_ref, route_ref, g_ref, ys_ref, o_ref, buf_ref, sem):
    tm = x2_ref.shape[0]

    def issue(t, c):
        for k in range(TOP_K):
            pltpu.make_async_copy(ys_ref.at[dest_ref[t * TOP_K + k]], buf_ref.at[k, t], sem).start()
        return c

    def drain(t, c):
        for k in range(TOP_K):
            pltpu.make_async_copy(ys_ref.at[dest_ref[t * TOP_K + k]], buf_ref.at[k, t], sem).wait()
        return c

    lax.fori_loop(0, tm, issue, 0)
    lax.fori_loop(0, tm, drain, 0)
    route = route_ref[...]
    x3 = x2_ref[...]
    for k in range(TOP_K):
        x3 = x3 + _load_row_tiles(buf_ref.at[k]) * route[:, TOP_K + k:TOP_K + k + 1]
    o_ref[...] = _rms(x3, g_ref[...])


def _combine(dest_flat, x2, route, g_final, ys):
    n, d = x2.shape
    tm = ROW_TILE
    return pl.pallas_call(
        _combine_kernel,
        grid=(n // tm,),
        in_specs=[pl.BlockSpec((tm * TOP_K,), lambda i: (i,), memory_space=pltpu.SMEM),
                  pl.BlockSpec((tm, d), lambda i: (i, 0)),
                  pl.BlockSpec((tm, LANES), lambda i: (i, 0)),
                  pl.BlockSpec((1, d), lambda i: (0, 0)),
                  pl.BlockSpec(memory_space=pl.ANY)],
        out_specs=pl.BlockSpec((tm, d), lambda i: (i, 0)),
        out_shape=jax.ShapeDtypeStruct((n, d), F32),
        scratch_shapes=[pltpu.VMEM((TOP_K, tm, d // LANES, LANES), F32), pltpu.SemaphoreType.DMA],
        compiler_params=pltpu.CompilerParams(
            dimension_semantics=("arbitrary",), vmem_limit_bytes=VMEM_LIMIT),
        name="moe_combine_norm",
    )(dest_flat, x2, route, g_final, ys)


def _layer(x, mem, g_mix, w_in, conv_w, g_conv_out, g_sb_out, w_out, g_xattn, g_mem,
           w_q_mem, w_kv_mem, w_o_mem, g_moe, w_router, b_router, w1, b1, w2, b2):
    b, s, d = x.shape
    n = b * s
    row2 = lambda g: g.reshape(1, -1)

    grp = jnp.arange(CONV_DIM) // CONV_GROUP_DIM
    gmat = jnp.where(grp[:, None] == grp[None, :], 1.0 / CONV_GROUP_DIM, 0.0).astype(BF16)
    ar = jnp.arange(SB_TILE)
    tri = (ar[:, None] > ar[None, :]).astype(BF16)
    ar = jnp.arange(ROW_TILE)
    ltri = (ar[:, None] > ar[None, :]).astype(BF16)

    kv = _mem_kv(mem, row2(g_mem), w_kv_mem.astype(BF16))
    yc, q, k, v = _inproj(x, row2(g_mix), w_in.astype(BF16), conv_w, row2(g_conv_out), gmat)
    ys = _stick_breaking(q, k, v, tri, row2(g_sb_out))
    w_out_b = w_out.astype(BF16)
    x2, h3, route, counts = _mid(x, yc, ys, w_out_b[:CONV_DIM], w_out_b[CONV_DIM:], row2(g_xattn),
                                 w_q_mem.astype(BF16), kv, w_o_mem.astype(BF16), row2(g_moe),
                                 w_router, row2(b_router), ltri)

    te = EXPERT_TILE
    route = route.reshape(n, LANES)
    counts = counts.reshape(N_EXPERTS).astype(jnp.int32)
    padded = (counts + te - 1) // te * te
    padded_end = jnp.cumsum(padded)
    start = padded_end - padded
    top_i = route[:, 0:TOP_K].astype(jnp.int32)
    rank = route[:, 2 * TOP_K:3 * TOP_K].astype(jnp.int32)
    experts = jnp.arange(N_EXPERTS, dtype=jnp.int32)
    first = jnp.sum(jnp.where(top_i[:, :, None] == experts, start, 0), axis=-1)
    dest = (first + rank).reshape(n * TOP_K)
    n_rows = n * TOP_K + N_EXPERTS * te
    n_blocks = n_rows // te
    n_used = (padded_end[-1] // te).astype(jnp.int32)
    blk = jnp.minimum(jnp.arange(n_blocks, dtype=jnp.int32), n_used - 1) * te
    block_e = jnp.minimum(jnp.sum((padded_end[None, :] <= blk[:, None]).astype(jnp.int32), axis=1),
                          N_EXPERTS - 1)

    xs = _dispatch(counts, padded, start, dest, h3, n_rows)
    yo = _experts(block_e, n_used.reshape(1), xs, w1, b1, w2, b2)
    return dest, x2.reshape(n, d), route, yo


def kernel(x, mem, g_mix, w_in, conv_w, g_conv_out, g_sb_out, w_out, g_xattn, g_mem, w_q_mem,
           w_kv_mem, w_o_mem, g_moe, w_router, b_router, w1, b1, w2, b2, g_final):
    b, s, d = x.shape
    assert w_in.shape[0] == 1, "single-layer stack"
    dest, x2, route, yo = _layer(x, mem, g_mix[0], w_in[0], conv_w[0], g_conv_out[0], g_sb_out[0],
                                 w_out[0], g_xattn[0], g_mem[0], w_q_mem[0], w_kv_mem[0],
                                 w_o_mem[0], g_moe[0], w_router[0], b_router[0],
                                 w1[0], b1[0], w2[0], b2[0])
    out = _combine(dest, x2, route, g_final.reshape(1, d), yo)
    return out.reshape(b, s, d)
```

```python
import jax
import jax.numpy as jnp
from jax import lax
from jax.experimental import pallas as pl
from jax.experimental.pallas import tpu as pltpu

F32 = jnp.float32
BF16 = jnp.bfloat16

EPS = 1e-5
LOG2E = 1.4426950408889634
CONV_DIM = 512
SB_DIM = 512
SB_HEAD_DIM = 64
CONV_GROUP_DIM = 64
X_HEADS = 4
N_EXPERTS = 32
TOP_K = 4
SWIGLU_LIMIT = 7.0
SWIGLU_ALPHA = 1.702

LANES = 128
ROW_TILE = 256
SB_TILE = 256
EXPERT_TILE = 256
EXPERT_CHUNK = 256
VMEM_LIMIT = 56 * 1024 * 1024


def _rms(x, g):
    return x * lax.rsqrt(jnp.mean(x * x, axis=-1, keepdims=True) + EPS) * g


def _split_dot(a, m):
    hi = a.astype(BF16)
    lo = (a - hi.astype(F32)).astype(BF16)
    return (jnp.dot(hi, m, preferred_element_type=F32)
            + jnp.dot(lo, m, preferred_element_type=F32))


def _mem_kv_kernel(mem_ref, g_ref, w_ref, kv_ref):
    h = _rms(mem_ref[0], g_ref[...]).astype(BF16)
    kv_ref[0] = jnp.dot(h, w_ref[...], preferred_element_type=F32).astype(BF16)


def _mem_kv(mem, g_mem, w_kv):
    b, m, d = mem.shape
    return pl.pallas_call(
        _mem_kv_kernel,
        grid=(b,),
        in_specs=[pl.BlockSpec((1, m, d), lambda i: (i, 0, 0)),
                  pl.BlockSpec((1, d), lambda i: (0, 0)),
                  pl.BlockSpec((d, 2 * d), lambda i: (0, 0))],
        out_specs=pl.BlockSpec((1, m, 2 * d), lambda i: (i, 0, 0)),
        out_shape=jax.ShapeDtypeStruct((b, m, 2 * d), BF16),
        compiler_params=pltpu.CompilerParams(
            dimension_semantics=("arbitrary",), vmem_limit_bytes=VMEM_LIMIT),
        name="mem_kv",
    )(mem, g_mem, w_kv)


def _inproj_kernel(x_ref, g_ref, w_ref, cw_ref, gc_ref, gmat_ref,
                   yc_ref, q_ref, k_ref, v_ref, prev_ref):
    tm = x_ref.shape[1]
    h = _rms(x_ref[0], g_ref[...]).astype(BF16)
    proj = jnp.dot(h, w_ref[...], preferred_element_type=F32)
    c = CONV_DIM
    u = proj[:, c:2 * c] * proj[:, 2 * c:3 * c]

    @pl.when(pl.program_id(1) == 0)
    def _():
        prev_ref[...] = jnp.zeros_like(prev_ref)

    rows = lax.broadcasted_iota(jnp.int32, (tm, 1), 0)
    p1 = prev_ref[7:8, :]
    p2 = prev_ref[6:7, :]
    u1 = jnp.where(rows == 0, p1, pltpu.roll(u, 1, axis=0))
    u2 = jnp.where(rows == 0, p2, jnp.where(rows == 1, p1, pltpu.roll(u, 2, axis=0)))
    prev_ref[...] = u[tm - 8:tm, :]
    y = proj[:, 0:c] * (cw_ref[0:1, :] * u2 + cw_ref[1:2, :] * u1 + cw_ref[2:3, :] * u)
    ms = _split_dot(y * y, gmat_ref[...])
    yc_ref[0] = (y * lax.rsqrt(ms + EPS) * gc_ref[...]).astype(BF16)
    o = 3 * c
    q_ref[0] = (proj[:, o:o + SB_DIM] * (SB_HEAD_DIM ** -0.5)).astype(BF16)
    k_ref[0] = proj[:, o + SB_DIM:o + 2 * SB_DIM].astype(BF16)
    v_ref[0] = proj[:, o + 2 * SB_DIM:o + 3 * SB_DIM].astype(BF16)


def _inproj(x, g_mix, w_in, conv_w, g_conv, gmat):
    b, s, d = x.shape
    tm = ROW_TILE
    p = w_in.shape[1]
    row = lambda i, j: (i, j, 0)
    fixed = lambda i, j: (0, 0)
    outs = [jax.ShapeDtypeStruct((b, s, CONV_DIM), BF16)] * 4
    return pl.pallas_call(
        _inproj_kernel,
        grid=(b, s // tm),
        in_specs=[pl.BlockSpec((1, tm, d), row),
                  pl.BlockSpec((1, d), fixed),
                  pl.BlockSpec((d, p), fixed),
                  pl.BlockSpec(conv_w.shape, fixed),
                  pl.BlockSpec((1, CONV_DIM), fixed),
                  pl.BlockSpec((CONV_DIM, CONV_DIM), fixed)],
        out_specs=[pl.BlockSpec((1, tm, CONV_DIM), row)] * 4,
        out_shape=outs,
        scratch_shapes=[pltpu.VMEM((8, CONV_DIM), F32)],
        compiler_params=pltpu.CompilerParams(
            dimension_semantics=("arbitrary", "arbitrary"), vmem_limit_bytes=VMEM_LIMIT),
        name="inproj_conv",
    )(x, g_mix, w_in, conv_w, g_conv, gmat)


def _sb_kernel(q_ref, k_ref, v_ref, tri_ref, g_ref, o_ref):
    tq = q_ref.shape[1]
    tk = tq
    qi = pl.program_id(2)
    q = q_ref[0]
    lane = lax.broadcasted_iota(jnp.int32, (1, LANES), 1)
    in_a = lane < SB_HEAD_DIM
    zero = jnp.zeros_like(q)
    q_heads = (jnp.where(in_a, q, zero), jnp.where(in_a, zero, q))
    tri = tri_ref[...]

    def scores(qh, kb):
        return lax.dot_general(qh, kb, (((1,), (1,)), ((), ())), preferred_element_type=F32) * LOG2E

    def later_keys(z2, valid):
        sp2 = jnp.maximum(z2, 0.0) + jnp.log2(1.0 + jnp.exp2(-jnp.abs(z2)))
        if valid is not None:
            sp2 = jnp.where(valid, sp2, 0.0)
        hi = sp2.astype(BF16)
        lo = (sp2 - hi.astype(F32)).astype(BF16)
        rest = jnp.dot(jnp.concatenate([hi, lo], axis=1), tri,
                       preferred_element_type=F32)
        return sp2, rest

    def weights(z2, sp2, rest, run, valid):
        a = jnp.exp2(z2 - sp2 - rest - run)
        if valid is not None:
            a = jnp.where(valid, a, 0.0)
        return a.astype(BF16), run + rest[:, 0:1] + sp2[:, 0:1]

    def tiles(js, carry, valid):
        run = [carry[0], carry[2]]
        acc = [carry[1], carry[3]]
        kbs, vbs = [], []
        for j in js:
            start = pl.multiple_of(j * tk, tk)
            kbs.append(k_ref[0, pl.ds(start, tk), :])
            vbs.append(v_ref[0, pl.ds(start, tk), :])
        zs = [[scores(qh, kb) for qh in q_heads] for kb in kbs]
        mids = [[later_keys(z, valid) for z in zt] for zt in zs]
        for t in range(len(js)):
            for h in range(2):
                w, run[h] = weights(zs[t][h], *mids[t][h], run[h], valid)
                acc[h] = acc[h] + jnp.dot(w, vbs[t], preferred_element_type=F32)
        return run[0], acc[0], run[1], acc[1]

    r0 = jnp.zeros((tq, 1), F32)
    a0 = jnp.zeros((tq, LANES), F32)
    t_idx = lax.broadcasted_iota(jnp.int32, (tq, tk), 0)
    s_idx = lax.broadcasted_iota(jnp.int32, (tq, tk), 1)
    carry = tiles([qi], (r0, a0, r0, a0), s_idx < t_idx)
    odd = qi % 2
    carry = lax.cond(odd == 1, lambda c: tiles([qi - 1], c, None), lambda c: c, carry)
    first = qi - 1 - odd
    carry = lax.fori_loop(0, qi // 2,
                          lambda n, c: tiles([first - 2 * n, first - 2 * n - 1], c, None), carry)
    out = jnp.where(in_a, carry[1], carry[3])
    sq = out * out
    ss_a = jnp.sum(jnp.where(in_a, sq, 0.0), axis=-1, keepdims=True)
    ss_b = jnp.sum(jnp.where(in_a, 0.0, sq), axis=-1, keepdims=True)
    ms = jnp.where(in_a, ss_a, ss_b) * (1.0 / SB_HEAD_DIM)
    o_ref[0] = (out * lax.rsqrt(ms + EPS) * g_ref[...]).astype(BF16)


def _stick_breaking(q, k, v, tri, g_sb):
    b, s, w = q.shape
    tq = SB_TILE
    return pl.pallas_call(
        _sb_kernel,
        grid=(b, w // LANES, s // tq),
        in_specs=[pl.BlockSpec((1, tq, LANES), lambda i, h, j: (i, j, h)),
                  pl.BlockSpec((1, s, LANES), lambda i, h, j: (i, 0, h)),
                  pl.BlockSpec((1, s, LANES), lambda i, h, j: (i, 0, h)),
                  pl.BlockSpec((2 * tq, tq), lambda i, h, j: (0, 0)),
                  pl.BlockSpec((1, LANES), lambda i, h, j: (0, h))],
        out_specs=pl.BlockSpec((1, tq, LANES), lambda i, h, j: (i, j, h)),
        out_shape=jax.ShapeDtypeStruct((b, s, w), BF16),
        compiler_params=pltpu.CompilerParams(
            dimension_semantics=("arbitrary", "arbitrary", "arbitrary"),
            vmem_limit_bytes=VMEM_LIMIT),
        name="stick_breaking",
    )(q, k, v, tri, g_sb)


def _mid_kernel(x_ref, yc_ref, ys_ref, wo1_ref, wo2_ref, gx_ref, wq_ref, kv_ref, wom_ref,
                gm_ref, wr_ref, br_ref, ltri_ref,
                x2_ref, h3_ref, route_ref, cnt_ref, carry_ref):
    tm = x_ref.shape[1]
    d = x_ref.shape[2]
    hd = d // X_HEADS

    @pl.when((pl.program_id(0) == 0) & (pl.program_id(1) == 0))
    def _():
        carry_ref[...] = jnp.zeros_like(carry_ref)

    x1 = (x_ref[0]
          + jnp.dot(yc_ref[0], wo1_ref[...], preferred_element_type=F32)
          + jnp.dot(ys_ref[0], wo2_ref[...], preferred_element_type=F32))
    h2 = _rms(x1, gx_ref[...]).astype(BF16)
    qm = (jnp.dot(h2, wq_ref[...], preferred_element_type=F32) * (hd ** -0.5)).astype(BF16)
    heads = []
    for h in range(X_HEADS):
        kh = kv_ref[0, :, h * hd:(h + 1) * hd]
        vh = kv_ref[0, :, d + h * hd:d + (h + 1) * hd]
        sc = lax.dot_general(qm[:, h * hd:(h + 1) * hd], kh, (((1,), (1,)), ((), ())),
                             preferred_element_type=F32)
        e = jnp.exp(sc - jnp.max(sc, axis=-1, keepdims=True))
        p = e / jnp.sum(e, axis=-1, keepdims=True)
        heads.append(jnp.dot(p.astype(BF16), vh, preferred_element_type=F32).astype(BF16))
    x2 = x1 + jnp.dot(jnp.concatenate(heads, axis=-1), wom_ref[...], preferred_element_type=F32)
    x2_ref[0] = x2
    h3 = _rms(x2, gm_ref[...])
    h3_ref[0] = h3

    logits = jnp.dot(h3, wr_ref[...], preferred_element_type=F32,
                     precision=lax.Precision.HIGHEST) + br_ref[...]
    lane = lax.broadcasted_iota(jnp.int32, (tm, N_EXPERTS), 1).astype(F32)
    vals, idxs = [], []
    l = logits
    for _ in range(TOP_K):
        m = jnp.max(l, axis=-1, keepdims=True)
        idx = jnp.min(jnp.where(l == m, lane, float(N_EXPERTS)), axis=-1, keepdims=True)
        vals.append(m)
        idxs.append(idx)
        l = jnp.where(lane == idx, -jnp.inf, l)
    es = [jnp.exp(vk - vals[0]) for vk in vals]
    den = es[0] + es[1] + es[2] + es[3]
    gates = [ek / den for ek in es]

    sel = jnp.zeros((tm, N_EXPERTS), F32)
    for idx in idxs:
        sel = sel + (lane == idx).astype(F32)
    before = jnp.dot(ltri_ref[...], sel.astype(BF16), preferred_element_type=F32) + carry_ref[...]
    ranks = [jnp.sum(jnp.where(lane == idx, before, 0.0), axis=-1, keepdims=True) for idx in idxs]
    carry_ref[...] += jnp.sum(sel, axis=0, keepdims=True)
    cnt_ref[...] = carry_ref[...]

    lane_o = lax.broadcasted_iota(jnp.int32, (tm, LANES), 1)
    r = jnp.zeros((tm, LANES), F32)
    for k in range(TOP_K):
        r = jnp.where(lane_o == k, idxs[k], r)
        r = jnp.where(lane_o == TOP_K + k, gates[k], r)
        r = jnp.where(lane_o == 2 * TOP_K + k, ranks[k], r)
    route_ref[0] = r


def _mid(x, yc, ys, wo1, wo2, g_x, w_q, kv, w_om, g_moe, w_r, b_r, ltri):
    b, s, d = x.shape
    tm = ROW_TILE
    m = kv.shape[1]
    row = lambda i, j: (i, j, 0)
    fixed = lambda i, j: (0, 0)
    return pl.pallas_call(
        _mid_kernel,
        grid=(b, s // tm),
        in_specs=[pl.BlockSpec((1, tm, d), row),
                  pl.BlockSpec((1, tm, CONV_DIM), row),
                  pl.BlockSpec((1, tm, SB_DIM), row),
                  pl.BlockSpec((CONV_DIM, d), fixed),
                  pl.BlockSpec((SB_DIM, d), fixed),
                  pl.BlockSpec((1, d), fixed),
                  pl.BlockSpec((d, d), fixed),
                  pl.BlockSpec((1, m, 2 * d), lambda i, j: (i, 0, 0)),
                  pl.BlockSpec((d, d), fixed),
                  pl.BlockSpec((1, d), fixed),
                  pl.BlockSpec((d, N_EXPERTS), fixed),
                  pl.BlockSpec((1, N_EXPERTS), fixed),
                  pl.BlockSpec((tm, tm), fixed)],
        out_specs=[pl.BlockSpec((1, tm, d), row),
                   pl.BlockSpec((1, tm, d), row),
                   pl.BlockSpec((1, tm, LANES), row),
                   pl.BlockSpec((1, N_EXPERTS), fixed)],
        out_shape=[jax.ShapeDtypeStruct((b, s, d), F32),
                   jax.ShapeDtypeStruct((b, s, d), F32),
                   jax.ShapeDtypeStruct((b, s, LANES), F32),
                   jax.ShapeDtypeStruct((1, N_EXPERTS), F32)],
        scratch_shapes=[pltpu.VMEM((1, N_EXPERTS), F32)],
        compiler_params=pltpu.CompilerParams(
            dimension_semantics=("arbitrary", "arbitrary"), vmem_limit_bytes=VMEM_LIMIT),
        name="outproj_xattn_router",
    )(x, yc, ys, wo1, wo2, g_x, w_q, kv, w_om, g_moe, w_r, b_r, ltri)


def _row_copy(src, src_row, dst, dst_row, sem):
    return pltpu.make_async_copy(src.at[pl.ds(src_row, 1), :], dst.at[pl.ds(dst_row, 1), :], sem)


def _dispatch_kernel(cnt_ref, pad_ref, start_ref, dest_ref, h_ref, zero_ref, xs_ref, sem, zsem):
    i = pl.program_id(0)
    n_assign = dest_ref.shape[0]
    tm = n_assign // TOP_K
    te = zero_ref.shape[0]

    def issue(t, c):
        for k in range(TOP_K):
            _row_copy(h_ref, t, xs_ref, dest_ref[t * TOP_K + k], sem).start()
        return c

    lax.fori_loop(0, tm, issue, 0, unroll=8)

    @pl.when(i == 0)
    def _():
        def per_expert(e, c):
            first = start_ref[e] + cnt_ref[e]
            n_pad = pad_ref[e] - cnt_ref[e]

            def pad_start(r, cc):
                _row_copy(zero_ref, 0, xs_ref, first + r, zsem).start()
                return cc

            def pad_wait(r, cc):
                _row_copy(zero_ref, 0, xs_ref, first + r, zsem).wait()
                return cc

            lax.fori_loop(0, n_pad, pad_start, 0)
            lax.fori_loop(0, n_pad, pad_wait, 0)
            return c

        lax.fori_loop(0, N_EXPERTS, per_expert, 0)
        last = N_EXPERTS - 1
        n_used = (start_ref[last] + pad_ref[last]) // te

        def tail_copy(j):
            rows = pl.ds(pl.multiple_of(j * te, te), te)
            return pltpu.make_async_copy(zero_ref, xs_ref.at[rows, :], zsem)

        def tail_start(j, c):
            tail_copy(j).start()
            return c

        def tail_wait(j, c):
            tail_copy(j).wait()
            return c

        lax.fori_loop(n_used, xs_ref.shape[0] // te, tail_start, 0)
        lax.fori_loop(n_used, xs_ref.shape[0] // te, tail_wait, 0)

    def drain(t, c):
        for k in range(TOP_K):
            _row_copy(h_ref, t, xs_ref, dest_ref[t * TOP_K + k], sem).wait()
        return c

    lax.fori_loop(0, tm, drain, 0)


def _dispatch(counts, padded, start, dest_flat, h3, n_rows):
    n, d = h3.shape
    tm = ROW_TILE
    zero = jnp.zeros((EXPERT_TILE, d), F32)
    return pl.pallas_call(
        _dispatch_kernel,
        grid_spec=pltpu.PrefetchScalarGridSpec(
            num_scalar_prefetch=3,
            grid=(n // tm,),
            in_specs=[pl.BlockSpec((tm * TOP_K,), lambda i, *_: (i,), memory_space=pltpu.SMEM),
                      pl.BlockSpec((tm, d), lambda i, *_: (i, 0)),
                      pl.BlockSpec(zero.shape, lambda i, *_: (0, 0))],
            out_specs=pl.BlockSpec(memory_space=pl.ANY),
            scratch_shapes=[pltpu.SemaphoreType.DMA, pltpu.SemaphoreType.DMA]),
        out_shape=jax.ShapeDtypeStruct((n_rows, d), F32),
        compiler_params=pltpu.CompilerParams(
            dimension_semantics=("arbitrary",), vmem_limit_bytes=VMEM_LIMIT),
        name="moe_dispatch",
    )(counts, padded, start, dest_flat, h3, zero)


def _expert_kernel(be_ref, nu_ref, xs_ref, w1_ref, b1_ref, w2_ref, b2_ref, o_ref, w1c_ref, w2c_ref):
    i = pl.program_id(0)
    de = w2_ref.shape[1]

    @pl.when(i >= nu_ref[0])
    def _():
        o_ref[...] = jnp.zeros_like(o_ref)

    @pl.when(i < nu_ref[0])
    def _():
        changed = (i == 0) | (be_ref[i] != be_ref[jnp.maximum(i - 1, 0)])

        @pl.when(changed)
        def _():
            w1c_ref[...] = w1_ref[0].astype(BF16)
            w2c_ref[...] = w2_ref[0].astype(BF16)

        xb = xs_ref[...].astype(BF16)
        ch = EXPERT_CHUNK

        def activation(c):
            g = (jnp.dot(xb, w1c_ref[:, c * ch:(c + 1) * ch], preferred_element_type=F32)
                 + b1_ref[0, :, c * ch:(c + 1) * ch])
            l = (jnp.dot(xb, w1c_ref[:, de + c * ch:de + (c + 1) * ch], preferred_element_type=F32)
                 + b1_ref[0, :, de + c * ch:de + (c + 1) * ch])
            glu = jnp.minimum(g, SWIGLU_LIMIT)
            lin = jnp.clip(l, -SWIGLU_LIMIT, SWIGLU_LIMIT)
            half = 0.5 * glu
            return ((half + half * jnp.tanh((0.5 * SWIGLU_ALPHA) * glu)) * (lin + 1.0)).astype(BF16)

        out = b2_ref[0]
        act = activation(0)
        for c in range(de // ch):
            nxt = activation(c + 1) if (c + 1) * ch < de else None
            out = out + jnp.dot(act, w2c_ref[c * ch:(c + 1) * ch, :], preferred_element_type=F32)
            act = nxt
        o_ref[...] = out


def _experts(block_e, n_used, xs, w1, b1, w2, b2):
    n_rows, d = xs.shape
    tm = EXPERT_TILE
    ne, _, dh = w1.shape
    de = w2.shape[1]
    rows = lambda i, be, nu: (jnp.maximum(jnp.minimum(i, nu[0] - 1), 0), 0)
    per_e = lambda i, be, nu: (be[i], 0, 0)
    return pl.pallas_call(
        _expert_kernel,
        grid_spec=pltpu.PrefetchScalarGridSpec(
            num_scalar_prefetch=2,
            grid=(n_rows // tm,),
            in_specs=[pl.BlockSpec((tm, d), rows),
                      pl.BlockSpec((1, d, dh), per_e),
                      pl.BlockSpec((1, 1, dh), per_e),
                      pl.BlockSpec((1, de, d), per_e),
                      pl.BlockSpec((1, 1, d), per_e)],
            out_specs=pl.BlockSpec((tm, d), lambda i, be, nu: (i, 0)),
            scratch_shapes=[pltpu.VMEM((d, dh), BF16), pltpu.VMEM((de, d), BF16)]),
        out_shape=jax.ShapeDtypeStruct(xs.shape, F32),
        compiler_params=pltpu.CompilerParams(
            dimension_semantics=("arbitrary",), vmem_limit_bytes=VMEM_LIMIT),
        name="moe_experts",
    )(block_e, n_used, xs, w1, b1.reshape(ne, 1, dh), w2, b2.reshape(ne, 1, d))


def _combine_kernel(dest_ref, x2_ref, route_ref, g_ref, ys_ref, o_ref, buf_ref, sem):
    tm = x2_ref.shape[0]

    def issue(t, c):
        for k in range(TOP_K):
            _row_copy(ys_ref, dest_ref[t * TOP_K + k], buf_ref.at[k], t, sem).start()
        return c

    def drain(t, c):
        for k in range(TOP_K):
            _row_copy(ys_ref, dest_ref[t * TOP_K + k], buf_ref.at[k], t, sem).wait()
        return c

    lax.fori_loop(0, tm, issue, 0, unroll=8)
    lax.fori_loop(0, tm, drain, 0)
    route = route_ref[...]
    x3 = x2_ref[...]
    for k in range(TOP_K):
        x3 = x3 + buf_ref[k] * route[:, TOP_K + k:TOP_K + k + 1]
    o_ref[...] = _rms(x3, g_ref[...])


def _combine(dest_flat, x2, route, g_final, ys):
    n, d = x2.shape
    tm = ROW_TILE
    return pl.pallas_call(
        _combine_kernel,
        grid=(n // tm,),
        in_specs=[pl.BlockSpec((tm * TOP_K,), lambda i: (i,), memory_space=pltpu.SMEM),
                  pl.BlockSpec((tm, d), lambda i: (i, 0)),
                  pl.BlockSpec((tm, LANES), lambda i: (i, 0)),
                  pl.BlockSpec((1, d), lambda i: (0, 0)),
                  pl.BlockSpec(memory_space=pl.ANY)],
        out_specs=pl.BlockSpec((tm, d), lambda i: (i, 0)),
        out_shape=jax.ShapeDtypeStruct((n, d), F32),
        scratch_shapes=[pltpu.VMEM((TOP_K, tm, d), F32), pltpu.SemaphoreType.DMA],
        compiler_params=pltpu.CompilerParams(
            dimension_semantics=("arbitrary",), vmem_limit_bytes=VMEM_LIMIT),
        name="moe_combine_norm",
    )(dest_flat, x2, route, g_final, ys)


def _layer(x, mem, g_mix, w_in, conv_w, g_conv_out, g_sb_out, w_out, g_xattn, g_mem,
           w_q_mem, w_kv_mem, w_o_mem, g_moe, w_router, b_router, w1, b1, w2, b2):
    b, s, d = x.shape
    n = b * s
    row2 = lambda g: g.reshape(1, -1)

    grp = jnp.arange(CONV_DIM) // CONV_GROUP_DIM
    gmat = jnp.where(grp[:, None] == grp[None, :], 1.0 / CONV_GROUP_DIM, 0.0).astype(BF16)
    ar = jnp.arange(SB_TILE)
    tri = (ar[:, None] > ar[None, :]).astype(BF16)
    tri = jnp.concatenate([tri, tri], axis=0)
    ar = jnp.arange(ROW_TILE)
    ltri = (ar[:, None] > ar[None, :]).astype(BF16)

    kv = _mem_kv(mem, row2(g_mem), w_kv_mem.astype(BF16))
    yc, q, k, v = _inproj(x, row2(g_mix), w_in.astype(BF16), conv_w, row2(g_conv_out), gmat)
    ys = _stick_breaking(q, k, v, tri, row2(g_sb_out))
    w_out_b = w_out.astype(BF16)
    x2, h3, route, counts = _mid(x, yc, ys, w_out_b[:CONV_DIM], w_out_b[CONV_DIM:], row2(g_xattn),
                                 w_q_mem.astype(BF16), kv, w_o_mem.astype(BF16), row2(g_moe),
                                 w_router, row2(b_router), ltri)

    te = EXPERT_TILE
    route = route.reshape(n, LANES)
    counts = counts.reshape(N_EXPERTS).astype(jnp.int32)
    padded = (counts + te - 1) // te * te
    padded_end = jnp.cumsum(padded)
    start = padded_end - padded
    top_i = route[:, 0:TOP_K].astype(jnp.int32)
    rank = route[:, 2 * TOP_K:3 * TOP_K].astype(jnp.int32)
    experts = jnp.arange(N_EXPERTS, dtype=jnp.int32)
    first = jnp.sum(jnp.where(top_i[:, :, None] == experts, start, 0), axis=-1)
    dest = (first + rank).reshape(n * TOP_K)
    n_rows = n * TOP_K + N_EXPERTS * te
    n_blocks = n_rows // te
    n_used = (padded_end[-1] // te).astype(jnp.int32)
    blk = jnp.minimum(jnp.arange(n_blocks, dtype=jnp.int32), n_used - 1) * te
    block_e = jnp.minimum(jnp.sum((padded_end[None, :] <= blk[:, None]).astype(jnp.int32), axis=1),
                          N_EXPERTS - 1)

    xs = _dispatch(counts, padded, start, dest, h3.reshape(n, d), n_rows)
    yo = _experts(block_e, n_used.reshape(1), xs, w1, b1, w2, b2)
    return dest, x2.reshape(n, d), route, yo


def kernel(x, mem, g_mix, w_in, conv_w, g_conv_out, g_sb_out, w_out, g_xattn, g_mem, w_q_mem,
           w_kv_mem, w_o_mem, g_moe, w_router, b_router, w1, b1, w2, b2, g_final):
    b, s, d = x.shape
    assert w_in.shape[0] == 1, "single-layer stack"
    dest, x2, route, yo = _layer(x, mem, g_mix[0], w_in[0], conv_w[0], g_conv_out[0], g_sb_out[0],
                                 w_out[0], g_xattn[0], g_mem[0], w_q_mem[0], w_kv_mem[0],
                                 w_o_mem[0], g_moe[0], w_router[0], b_router[0],
                                 w1[0], b1[0], w2[0], b2[0])
    out = _combine(dest, x2, route, g_final.reshape(1, d), yo)
    return out.reshape(b, s, d)
```

```python
import jax
import jax.numpy as jnp
from jax import lax
from jax.experimental import pallas as pl
from jax.experimental.pallas import tpu as pltpu

F32 = jnp.float32
BF16 = jnp.bfloat16

EPS = 1e-5
LOG2E = 1.4426950408889634
CONV_DIM = 512
SB_DIM = 512
SB_HEAD_DIM = 64
CONV_GROUP_DIM = 64
X_HEADS = 4
N_EXPERTS = 32
TOP_K = 4
SWIGLU_LIMIT = 7.0
SWIGLU_ALPHA = 1.702

LANES = 128
ROW_TILE = 256
SB_TILE = 256
EXPERT_TILE = 512
EXPERT_CHUNK = 256
VMEM_LIMIT = 56 * 1024 * 1024


def _rms(x, g):
    return x * lax.rsqrt(jnp.mean(x * x, axis=-1, keepdims=True) + EPS) * g


def _split_dot(a, m):
    hi = a.astype(BF16)
    lo = (a - hi.astype(F32)).astype(BF16)
    return (jnp.dot(hi, m, preferred_element_type=F32)
            + jnp.dot(lo, m, preferred_element_type=F32))


def _mem_kv_kernel(mem_ref, g_ref, w_ref, kv_ref):
    h = _rms(mem_ref[0], g_ref[...]).astype(BF16)
    kv_ref[0] = jnp.dot(h, w_ref[...], preferred_element_type=F32).astype(BF16)


def _mem_kv(mem, g_mem, w_kv):
    b, m, d = mem.shape
    return pl.pallas_call(
        _mem_kv_kernel,
        grid=(b,),
        in_specs=[pl.BlockSpec((1, m, d), lambda i: (i, 0, 0)),
                  pl.BlockSpec((1, d), lambda i: (0, 0)),
                  pl.BlockSpec((d, 2 * d), lambda i: (0, 0))],
        out_specs=pl.BlockSpec((1, m, 2 * d), lambda i: (i, 0, 0)),
        out_shape=jax.ShapeDtypeStruct((b, m, 2 * d), BF16),
        compiler_params=pltpu.CompilerParams(
            dimension_semantics=("arbitrary",), vmem_limit_bytes=VMEM_LIMIT),
        name="mem_kv",
    )(mem, g_mem, w_kv)


def _inproj_kernel(x_ref, g_ref, w_ref, cw_ref, gc_ref, gmat_ref,
                   yc_ref, q_ref, k_ref, v_ref, prev_ref):
    tm = x_ref.shape[1]
    h = _rms(x_ref[0], g_ref[...]).astype(BF16)
    proj = jnp.dot(h, w_ref[...], preferred_element_type=F32)
    c = CONV_DIM
    u = proj[:, c:2 * c] * proj[:, 2 * c:3 * c]

    @pl.when(pl.program_id(1) == 0)
    def _():
        prev_ref[...] = jnp.zeros_like(prev_ref)

    rows = lax.broadcasted_iota(jnp.int32, (tm, 1), 0)
    p1 = prev_ref[7:8, :]
    p2 = prev_ref[6:7, :]
    u1 = jnp.where(rows == 0, p1, pltpu.roll(u, 1, axis=0))
    u2 = jnp.where(rows == 0, p2, jnp.where(rows == 1, p1, pltpu.roll(u, 2, axis=0)))
    prev_ref[...] = u[tm - 8:tm, :]
    y = proj[:, 0:c] * (cw_ref[0:1, :] * u2 + cw_ref[1:2, :] * u1 + cw_ref[2:3, :] * u)
    ms = _split_dot(y * y, gmat_ref[...])
    yc_ref[0] = (y * lax.rsqrt(ms + EPS) * gc_ref[...]).astype(BF16)
    o = 3 * c
    q_ref[0] = (proj[:, o:o + SB_DIM] * (SB_HEAD_DIM ** -0.5)).astype(BF16)
    k_ref[0] = proj[:, o + SB_DIM:o + 2 * SB_DIM].astype(BF16)
    v_ref[0] = proj[:, o + 2 * SB_DIM:o + 3 * SB_DIM].astype(BF16)


def _inproj(x, g_mix, w_in, conv_w, g_conv, gmat):
    b, s, d = x.shape
    tm = ROW_TILE
    p = w_in.shape[1]
    row = lambda i, j: (i, j, 0)
    fixed = lambda i, j: (0, 0)
    outs = [jax.ShapeDtypeStruct((b, s, CONV_DIM), BF16)] * 4
    return pl.pallas_call(
        _inproj_kernel,
        grid=(b, s // tm),
        in_specs=[pl.BlockSpec((1, tm, d), row),
                  pl.BlockSpec((1, d), fixed),
                  pl.BlockSpec((d, p), fixed),
                  pl.BlockSpec(conv_w.shape, fixed),
                  pl.BlockSpec((1, CONV_DIM), fixed),
                  pl.BlockSpec((CONV_DIM, CONV_DIM), fixed)],
        out_specs=[pl.BlockSpec((1, tm, CONV_DIM), row)] * 4,
        out_shape=outs,
        scratch_shapes=[pltpu.VMEM((8, CONV_DIM), F32)],
        compiler_params=pltpu.CompilerParams(
            dimension_semantics=("arbitrary", "arbitrary"), vmem_limit_bytes=VMEM_LIMIT),
        name="inproj_conv",
    )(x, g_mix, w_in, conv_w, g_conv, gmat)


def _sb_kernel(q_ref, k_ref, v_ref, tri_ref, g_ref, o_ref):
    tq = q_ref.shape[1]
    tk = tq
    qi = pl.program_id(2)
    q = q_ref[0]
    lane = lax.broadcasted_iota(jnp.int32, (1, LANES), 1)
    in_a = lane < SB_HEAD_DIM
    zero = jnp.zeros_like(q)
    q_heads = (jnp.where(in_a, q, zero), jnp.where(in_a, zero, q))
    tri = tri_ref[...]

    def scores(qh, kb):
        return lax.dot_general(qh, kb, (((1,), (1,)), ((), ())), preferred_element_type=F32) * LOG2E

    def later_keys(z2, valid):
        sp2 = jnp.maximum(z2, 0.0) + jnp.log2(1.0 + jnp.exp2(-jnp.abs(z2)))
        if valid is not None:
            sp2 = jnp.where(valid, sp2, 0.0)
        hi = sp2.astype(BF16)
        lo = (sp2 - hi.astype(F32)).astype(BF16)
        rest = jnp.dot(jnp.concatenate([hi, lo], axis=1), tri,
                       preferred_element_type=F32)
        return sp2, rest

    def weights(z2, sp2, rest, run, valid):
        a = jnp.exp2(z2 - sp2 - rest - run)
        if valid is not None:
            a = jnp.where(valid, a, 0.0)
        return a.astype(BF16), run + rest[:, 0:1] + sp2[:, 0:1]

    def tiles(js, carry, valid):
        run = [carry[0], carry[2]]
        acc = [carry[1], carry[3]]
        kbs, vbs = [], []
        for j in js:
            start = pl.multiple_of(j * tk, tk)
            kbs.append(k_ref[0, pl.ds(start, tk), :])
            vbs.append(v_ref[0, pl.ds(start, tk), :])
        zs = [[scores(qh, kb) for qh in q_heads] for kb in kbs]
        mids = [[later_keys(z, valid) for z in zt] for zt in zs]
        for t in range(len(js)):
            for h in range(2):
                w, run[h] = weights(zs[t][h], *mids[t][h], run[h], valid)
                acc[h] = acc[h] + jnp.dot(w, vbs[t], preferred_element_type=F32)
        return run[0], acc[0], run[1], acc[1]

    r0 = jnp.zeros((tq, 1), F32)
    a0 = jnp.zeros((tq, LANES), F32)
    t_idx = lax.broadcasted_iota(jnp.int32, (tq, tk), 0)
    s_idx = lax.broadcasted_iota(jnp.int32, (tq, tk), 1)
    carry = tiles([qi], (r0, a0, r0, a0), s_idx < t_idx)
    odd = qi % 2
    carry = lax.cond(odd == 1, lambda c: tiles([qi - 1], c, None), lambda c: c, carry)
    first = qi - 1 - odd
    carry = lax.fori_loop(0, qi // 2,
                          lambda n, c: tiles([first - 2 * n, first - 2 * n - 1], c, None), carry)
    out = jnp.where(in_a, carry[1], carry[3])
    sq = out * out
    ss_a = jnp.sum(jnp.where(in_a, sq, 0.0), axis=-1, keepdims=True)
    ss_b = jnp.sum(jnp.where(in_a, 0.0, sq), axis=-1, keepdims=True)
    ms = jnp.where(in_a, ss_a, ss_b) * (1.0 / SB_HEAD_DIM)
    o_ref[0] = (out * lax.rsqrt(ms + EPS) * g_ref[...]).astype(BF16)


def _stick_breaking(q, k, v, tri, g_sb):
    b, s, w = q.shape
    tq = SB_TILE
    return pl.pallas_call(
        _sb_kernel,
        grid=(b, w // LANES, s // tq),
        in_specs=[pl.BlockSpec((1, tq, LANES), lambda i, h, j: (i, j, h)),
                  pl.BlockSpec((1, s, LANES), lambda i, h, j: (i, 0, h)),
                  pl.BlockSpec((1, s, LANES), lambda i, h, j: (i, 0, h)),
                  pl.BlockSpec((2 * tq, tq), lambda i, h, j: (0, 0)),
                  pl.BlockSpec((1, LANES), lambda i, h, j: (0, h))],
        out_specs=pl.BlockSpec((1, tq, LANES), lambda i, h, j: (i, j, h)),
        out_shape=jax.ShapeDtypeStruct((b, s, w), BF16),
        compiler_params=pltpu.CompilerParams(
            dimension_semantics=("arbitrary", "arbitrary", "arbitrary"),
            vmem_limit_bytes=VMEM_LIMIT),
        name="stick_breaking",
    )(q, k, v, tri, g_sb)


def _mid_kernel(x_ref, yc_ref, ys_ref, wo1_ref, wo2_ref, gx_ref, wq_ref, kv_ref, wom_ref,
                gm_ref, wr_ref, br_ref, ltri_ref,
                x2_ref, h3_ref, route_ref, cnt_ref, carry_ref):
    tm = x_ref.shape[1]
    d = x_ref.shape[2]
    hd = d // X_HEADS

    @pl.when((pl.program_id(0) == 0) & (pl.program_id(1) == 0))
    def _():
        carry_ref[...] = jnp.zeros_like(carry_ref)

    x1 = (x_ref[0]
          + jnp.dot(yc_ref[0], wo1_ref[...], preferred_element_type=F32)
          + jnp.dot(ys_ref[0], wo2_ref[...], preferred_element_type=F32))
    h2 = _rms(x1, gx_ref[...]).astype(BF16)
    qm = (jnp.dot(h2, wq_ref[...], preferred_element_type=F32) * (hd ** -0.5)).astype(BF16)
    heads = []
    for h in range(X_HEADS):
        kh = kv_ref[0, :, h * hd:(h + 1) * hd]
        vh = kv_ref[0, :, d + h * hd:d + (h + 1) * hd]
        sc = lax.dot_general(qm[:, h * hd:(h + 1) * hd], kh, (((1,), (1,)), ((), ())),
                             preferred_element_type=F32)
        e = jnp.exp(sc - jnp.max(sc, axis=-1, keepdims=True))
        p = e / jnp.sum(e, axis=-1, keepdims=True)
        heads.append(jnp.dot(p.astype(BF16), vh, preferred_element_type=F32).astype(BF16))
    x2 = x1 + jnp.dot(jnp.concatenate(heads, axis=-1), wom_ref[...], preferred_element_type=F32)
    x2_ref[0] = x2
    h3 = _rms(x2, gm_ref[...])
    h3_ref[0] = h3

    h_hi = h3.astype(BF16)
    h_lo = (h3 - h_hi.astype(F32)).astype(BF16)
    logits = jnp.dot(jnp.concatenate([h_hi, h_lo, h_hi], axis=1), wr_ref[...],
                     preferred_element_type=F32) + br_ref[...]
    lane = lax.broadcasted_iota(jnp.int32, (tm, N_EXPERTS), 1).astype(F32)
    vals, idxs = [], []
    l = logits
    for _ in range(TOP_K):
        m = jnp.max(l, axis=-1, keepdims=True)
        idx = jnp.min(jnp.where(l == m, lane, float(N_EXPERTS)), axis=-1, keepdims=True)
        vals.append(m)
        idxs.append(idx)
        l = jnp.where(lane == idx, -jnp.inf, l)
    es = [jnp.exp(vk - vals[0]) for vk in vals]
    den = es[0] + es[1] + es[2] + es[3]
    gates = [ek / den for ek in es]

    sel = jnp.zeros((tm, N_EXPERTS), F32)
    for idx in idxs:
        sel = sel + (lane == idx).astype(F32)
    before = jnp.dot(ltri_ref[...], sel.astype(BF16), preferred_element_type=F32) + carry_ref[...]
    ranks = [jnp.sum(jnp.where(lane == idx, before, 0.0), axis=-1, keepdims=True) for idx in idxs]
    carry_ref[...] += jnp.sum(sel, axis=0, keepdims=True)
    cnt_ref[...] = carry_ref[...]

    lane_o = lax.broadcasted_iota(jnp.int32, (tm, LANES), 1)
    r = jnp.zeros((tm, LANES), F32)
    for k in range(TOP_K):
        r = jnp.where(lane_o == k, idxs[k], r)
        r = jnp.where(lane_o == TOP_K + k, gates[k], r)
        r = jnp.where(lane_o == 2 * TOP_K + k, ranks[k], r)
    route_ref[0] = r


def _mid(x, yc, ys, wo1, wo2, g_x, w_q, kv, w_om, g_moe, w_r, b_r, ltri):
    b, s, d = x.shape
    tm = ROW_TILE
    m = kv.shape[1]
    row = lambda i, j: (i, j, 0)
    fixed = lambda i, j: (0, 0)
    return pl.pallas_call(
        _mid_kernel,
        grid=(b, s // tm),
        in_specs=[pl.BlockSpec((1, tm, d), row),
                  pl.BlockSpec((1, tm, CONV_DIM), row),
                  pl.BlockSpec((1, tm, SB_DIM), row),
                  pl.BlockSpec((CONV_DIM, d), fixed),
                  pl.BlockSpec((SB_DIM, d), fixed),
                  pl.BlockSpec((1, d), fixed),
                  pl.BlockSpec((d, d), fixed),
                  pl.BlockSpec((1, m, 2 * d), lambda i, j: (i, 0, 0)),
                  pl.BlockSpec((d, d), fixed),
                  pl.BlockSpec((1, d), fixed),
                  pl.BlockSpec((3 * d, N_EXPERTS), fixed),
                  pl.BlockSpec((1, N_EXPERTS), fixed),
                  pl.BlockSpec((tm, tm), fixed)],
        out_specs=[pl.BlockSpec((1, tm, d), row),
                   pl.BlockSpec((1, tm, d), row),
                   pl.BlockSpec((1, tm, LANES), row),
                   pl.BlockSpec((1, N_EXPERTS), fixed)],
        out_shape=[jax.ShapeDtypeStruct((b, s, d), F32),
                   jax.ShapeDtypeStruct((b, s, d), F32),
                   jax.ShapeDtypeStruct((b, s, LANES), F32),
                   jax.ShapeDtypeStruct((1, N_EXPERTS), F32)],
        scratch_shapes=[pltpu.VMEM((1, N_EXPERTS), F32)],
        compiler_params=pltpu.CompilerParams(
            dimension_semantics=("arbitrary", "arbitrary"), vmem_limit_bytes=VMEM_LIMIT),
        name="outproj_xattn_router",
    )(x, yc, ys, wo1, wo2, g_x, w_q, kv, w_om, g_moe, w_r, b_r, ltri)


def _row_copy(src, src_row, dst, dst_row, sem):
    return pltpu.make_async_copy(src.at[pl.ds(src_row, 1), :], dst.at[pl.ds(dst_row, 1), :], sem)


def _dispatch_kernel(cnt_ref, pad_ref, start_ref, dest_ref, h_ref, zero_ref, xs_ref, sem, zsem):
    i = pl.program_id(0)
    n_assign = dest_ref.shape[0]
    tm = n_assign // TOP_K
    te = zero_ref.shape[0]

    def issue(t, c):
        for k in range(TOP_K):
            _row_copy(h_ref, t, xs_ref, dest_ref[t * TOP_K + k], sem).start(priority=k % 2)
        return c

    lax.fori_loop(0, tm, issue, 0, unroll=8)

    @pl.when(i == 0)
    def _():
        def per_expert(e, c):
            first = start_ref[e] + cnt_ref[e]
            n_pad = pad_ref[e] - cnt_ref[e]

            def pad_start(r, cc):
                _row_copy(zero_ref, 0, xs_ref, first + r, zsem).start()
                return cc

            def pad_wait(r, cc):
                _row_copy(zero_ref, 0, xs_ref, first + r, zsem).wait()
                return cc

            lax.fori_loop(0, n_pad, pad_start, 0)
            lax.fori_loop(0, n_pad, pad_wait, 0)
            return c

        lax.fori_loop(0, N_EXPERTS, per_expert, 0)
        last = N_EXPERTS - 1
        n_used = (start_ref[last] + pad_ref[last]) // te

        def tail_copy(j):
            rows = pl.ds(pl.multiple_of(j * te, te), te)
            return pltpu.make_async_copy(zero_ref, xs_ref.at[rows, :], zsem)

        def tail_start(j, c):
            tail_copy(j).start()
            return c

        def tail_wait(j, c):
            tail_copy(j).wait()
            return c

        lax.fori_loop(n_used, xs_ref.shape[0] // te, tail_start, 0)
        lax.fori_loop(n_used, xs_ref.shape[0] // te, tail_wait, 0)

    def drain(t, c):
        for k in range(TOP_K):
            _row_copy(h_ref, t, xs_ref, dest_ref[t * TOP_K + k], sem).wait()
        return c

    lax.fori_loop(0, tm, drain, 0)


def _dispatch(counts, padded, start, dest_flat, h3, n_rows):
    n, d = h3.shape
    tm = ROW_TILE
    zero = jnp.zeros((EXPERT_TILE, d), F32)
    return pl.pallas_call(
        _dispatch_kernel,
        grid_spec=pltpu.PrefetchScalarGridSpec(
            num_scalar_prefetch=3,
            grid=(n // tm,),
            in_specs=[pl.BlockSpec((tm * TOP_K,), lambda i, *_: (i,), memory_space=pltpu.SMEM),
                      pl.BlockSpec((tm, d), lambda i, *_: (i, 0)),
                      pl.BlockSpec(zero.shape, lambda i, *_: (0, 0))],
            out_specs=pl.BlockSpec(memory_space=pl.ANY),
            scratch_shapes=[pltpu.SemaphoreType.DMA, pltpu.SemaphoreType.DMA]),
        out_shape=jax.ShapeDtypeStruct((n_rows, d), F32),
        compiler_params=pltpu.CompilerParams(
            dimension_semantics=("arbitrary",), vmem_limit_bytes=VMEM_LIMIT),
        name="moe_dispatch",
    )(counts, padded, start, dest_flat, h3, zero)


def _expert_kernel(be_ref, nu_ref, xs_ref, w1_ref, b1_ref, w2_ref, b2_ref, o_ref, w1c_ref, w2c_ref):
    i = pl.program_id(0)
    de = w2_ref.shape[1]

    @pl.when(i >= nu_ref[0])
    def _():
        o_ref[...] = jnp.zeros_like(o_ref)

    @pl.when(i < nu_ref[0])
    def _():
        changed = (i == 0) | (be_ref[i] != be_ref[jnp.maximum(i - 1, 0)])

        @pl.when(changed)
        def _():
            w1c_ref[...] = w1_ref[0].astype(BF16)
            w2c_ref[...] = w2_ref[0].astype(BF16)

        xb = xs_ref[...].astype(BF16)
        ch = EXPERT_CHUNK

        def activation(c):
            g = (jnp.dot(xb, w1c_ref[:, c * ch:(c + 1) * ch], preferred_element_type=F32)
                 + b1_ref[0, :, c * ch:(c + 1) * ch])
            l = (jnp.dot(xb, w1c_ref[:, de + c * ch:de + (c + 1) * ch], preferred_element_type=F32)
                 + b1_ref[0, :, de + c * ch:de + (c + 1) * ch])
            glu = jnp.minimum(g, SWIGLU_LIMIT)
            lin = jnp.clip(l, -SWIGLU_LIMIT, SWIGLU_LIMIT)
            half = 0.5 * glu
            return ((half + half * jnp.tanh((0.5 * SWIGLU_ALPHA) * glu)) * (lin + 1.0)).astype(BF16)

        out = b2_ref[0]
        act = activation(0)
        for c in range(de // ch):
            nxt = activation(c + 1) if (c + 1) * ch < de else None
            out = out + jnp.dot(act, w2c_ref[c * ch:(c + 1) * ch, :], preferred_element_type=F32)
            act = nxt
        o_ref[...] = out


def _experts(block_e, n_used, xs, w1, b1, w2, b2):
    n_rows, d = xs.shape
    tm = EXPERT_TILE
    ne, _, dh = w1.shape
    de = w2.shape[1]
    rows = lambda i, be, nu: (jnp.maximum(jnp.minimum(i, nu[0] - 1), 0), 0)
    per_e = lambda i, be, nu: (be[i], 0, 0)
    return pl.pallas_call(
        _expert_kernel,
        grid_spec=pltpu.PrefetchScalarGridSpec(
            num_scalar_prefetch=2,
            grid=(n_rows // tm,),
            in_specs=[pl.BlockSpec((tm, d), rows),
                      pl.BlockSpec((1, d, dh), per_e),
                      pl.BlockSpec((1, 1, dh), per_e),
                      pl.BlockSpec((1, de, d), per_e),
                      pl.BlockSpec((1, 1, d), per_e)],
            out_specs=pl.BlockSpec((tm, d), lambda i, be, nu: (i, 0)),
            scratch_shapes=[pltpu.VMEM((d, dh), BF16), pltpu.VMEM((de, d), BF16)]),
        out_shape=jax.ShapeDtypeStruct(xs.shape, F32),
        compiler_params=pltpu.CompilerParams(
            dimension_semantics=("arbitrary",), vmem_limit_bytes=VMEM_LIMIT),
        name="moe_experts",
    )(block_e, n_used, xs, w1, b1.reshape(ne, 1, dh), w2, b2.reshape(ne, 1, d))


def _combine_kernel(dest_ref, x2_ref, route_ref, g_ref, ys_ref, o_ref, buf_ref, sem):
    tm = x2_ref.shape[0]

    def issue(t, c):
        for k in range(TOP_K):
            _row_copy(ys_ref, dest_ref[t * TOP_K + k], buf_ref.at[k], t, sem).start(priority=k % 2)
        return c

    def drain(t, c):
        for k in range(TOP_K):
            _row_copy(ys_ref, dest_ref[t * TOP_K + k], buf_ref.at[k], t, sem).wait()
        return c

    lax.fori_loop(0, tm, issue, 0, unroll=8)
    lax.fori_loop(0, tm, drain, 0)
    route = route_ref[...]
    x3 = x2_ref[...]
    for k in range(TOP_K):
        x3 = x3 + buf_ref[k] * route[:, TOP_K + k:TOP_K + k + 1]
    o_ref[...] = _rms(x3, g_ref[...])


def _combine(dest_flat, x2, route, g_final, ys):
    n, d = x2.shape
    tm = ROW_TILE
    return pl.pallas_call(
        _combine_kernel,
        grid=(n // tm,),
        in_specs=[pl.BlockSpec((tm * TOP_K,), lambda i: (i,), memory_space=pltpu.SMEM),
                  pl.BlockSpec((tm, d), lambda i: (i, 0)),
                  pl.BlockSpec((tm, LANES), lambda i: (i, 0)),
                  pl.BlockSpec((1, d), lambda i: (0, 0)),
                  pl.BlockSpec(memory_space=pl.ANY)],
        out_specs=pl.BlockSpec((tm, d), lambda i: (i, 0)),
        out_shape=jax.ShapeDtypeStruct((n, d), F32),
        scratch_shapes=[pltpu.VMEM((TOP_K, tm, d), F32), pltpu.SemaphoreType.DMA],
        compiler_params=pltpu.CompilerParams(
            dimension_semantics=("arbitrary",), vmem_limit_bytes=VMEM_LIMIT),
        name="moe_combine_norm",
    )(dest_flat, x2, route, g_final, ys)


def _layer(x, mem, g_mix, w_in, conv_w, g_conv_out, g_sb_out, w_out, g_xattn, g_mem,
           w_q_mem, w_kv_mem, w_o_mem, g_moe, w_router, b_router, w1, b1, w2, b2):
    b, s, d = x.shape
    n = b * s
    row2 = lambda g: g.reshape(1, -1)

    grp = jnp.arange(CONV_DIM) // CONV_GROUP_DIM
    gmat = jnp.where(grp[:, None] == grp[None, :], 1.0 / CONV_GROUP_DIM, 0.0).astype(BF16)
    ar = jnp.arange(SB_TILE)
    tri = (ar[:, None] > ar[None, :]).astype(BF16)
    tri = jnp.concatenate([tri, tri], axis=0)
    ar = jnp.arange(ROW_TILE)
    ltri = (ar[:, None] > ar[None, :]).astype(BF16)

    kv = _mem_kv(mem, row2(g_mem), w_kv_mem.astype(BF16))
    yc, q, k, v = _inproj(x, row2(g_mix), w_in.astype(BF16), conv_w, row2(g_conv_out), gmat)
    ys = _stick_breaking(q, k, v, tri, row2(g_sb_out))
    w_out_b = w_out.astype(BF16)
    wr_hi = w_router.astype(BF16)
    wr_lo = (w_router - wr_hi.astype(F32)).astype(BF16)
    x2, h3, route, counts = _mid(x, yc, ys, w_out_b[:CONV_DIM], w_out_b[CONV_DIM:], row2(g_xattn),
                                 w_q_mem.astype(BF16), kv, w_o_mem.astype(BF16), row2(g_moe),
                                 jnp.concatenate([wr_hi, wr_hi, wr_lo], axis=0), row2(b_router), ltri)

    te = EXPERT_TILE
    route = route.reshape(n, LANES)
    counts = counts.reshape(N_EXPERTS).astype(jnp.int32)
    padded = (counts + te - 1) // te * te
    padded_end = jnp.cumsum(padded)
    start = padded_end - padded
    top_i = route[:, 0:TOP_K].astype(jnp.int32)
    rank = route[:, 2 * TOP_K:3 * TOP_K].astype(jnp.int32)
    experts = jnp.arange(N_EXPERTS, dtype=jnp.int32)
    first = jnp.sum(jnp.where(top_i[:, :, None] == experts, start, 0), axis=-1)
    dest = (first + rank).reshape(n * TOP_K)
    n_rows = n * TOP_K + N_EXPERTS * te
    n_blocks = n_rows // te
    n_used = (padded_end[-1] // te).astype(jnp.int32)
    blk = jnp.minimum(jnp.arange(n_blocks, dtype=jnp.int32), n_used - 1) * te
    block_e = jnp.minimum(jnp.sum((padded_end[None, :] <= blk[:, None]).astype(jnp.int32), axis=1),
                          N_EXPERTS - 1)

    xs = _dispatch(counts, padded, start, dest, h3.reshape(n, d), n_rows)
    yo = _experts(block_e, n_used.reshape(1), xs, w1, b1, w2, b2)
    return dest, x2.reshape(n, d), route, yo


def kernel(x, mem, g_mix, w_in, conv_w, g_conv_out, g_sb_out, w_out, g_xattn, g_mem, w_q_mem,
           w_kv_mem, w_o_mem, g_moe, w_router, b_router, w1, b1, w2, b2, g_final):
    b, s, d = x.shape
    assert w_in.shape[0] == 1, "single-layer stack"
    dest, x2, route, yo = _layer(x, mem, g_mix[0], w_in[0], conv_w[0], g_conv_out[0], g_sb_out[0],
                                 w_out[0], g_xattn[0], g_mem[0], w_q_mem[0], w_kv_mem[0],
                                 w_o_mem[0], g_moe[0], w_router[0], b_router[0],
                                 w1[0], b1[0], w2[0], b2[0])
    out = _combine(dest, x2, route, g_final.reshape(1, d), yo)
    return out.reshape(b, s, d)
```

```python
import jax
import jax.numpy as jnp
from jax import lax
from jax.experimental import pallas as pl
from jax.experimental.pallas import tpu as pltpu

F32 = jnp.float32
BF16 = jnp.bfloat16

EPS = 1e-5
LOG2E = 1.4426950408889634
CONV_DIM = 512
SB_DIM = 512
SB_HEAD_DIM = 64
CONV_GROUP_DIM = 64
X_HEADS = 4
N_EXPERTS = 32
TOP_K = 4
SWIGLU_LIMIT = 7.0
SWIGLU_ALPHA = 1.702

LANES = 128
ROW_TILE = 256
SB_TILE = 256
EXPERT_TILE = 512
EXPERT_CHUNK = 256
VMEM_LIMIT = 56 * 1024 * 1024


def _rms(x, g):
    return x * lax.rsqrt(jnp.mean(x * x, axis=-1, keepdims=True) + EPS) * g


def _split_dot(a, m):
    hi = a.astype(BF16)
    lo = (a - hi.astype(F32)).astype(BF16)
    return (jnp.dot(hi, m, preferred_element_type=F32)
            + jnp.dot(lo, m, preferred_element_type=F32))


def _mem_kv_kernel(mem_ref, g_ref, w_ref, kv_ref):
    h = _rms(mem_ref[0], g_ref[...]).astype(BF16)
    kv_ref[0] = jnp.dot(h, w_ref[...], preferred_element_type=F32).astype(BF16)


def _mem_kv(mem, g_mem, w_kv):
    b, m, d = mem.shape
    return pl.pallas_call(
        _mem_kv_kernel,
        grid=(b,),
        in_specs=[pl.BlockSpec((1, m, d), lambda i: (i, 0, 0)),
                  pl.BlockSpec((1, d), lambda i: (0, 0)),
                  pl.BlockSpec((d, 2 * d), lambda i: (0, 0))],
        out_specs=pl.BlockSpec((1, m, 2 * d), lambda i: (i, 0, 0)),
        out_shape=jax.ShapeDtypeStruct((b, m, 2 * d), BF16),
        compiler_params=pltpu.CompilerParams(
            dimension_semantics=("arbitrary",), vmem_limit_bytes=VMEM_LIMIT),
        name="mem_kv",
    )(mem, g_mem, w_kv)


def _inproj_kernel(x_ref, g_ref, w_ref, cw_ref, gc_ref, gmat_ref,
                   yc_ref, q_ref, k_ref, v_ref, prev_ref):
    tm = x_ref.shape[1]
    h = _rms(x_ref[0], g_ref[...]).astype(BF16)
    proj = jnp.dot(h, w_ref[...], preferred_element_type=F32)
    c = CONV_DIM
    u = proj[:, c:2 * c] * proj[:, 2 * c:3 * c]

    @pl.when(pl.program_id(1) == 0)
    def _():
        prev_ref[...] = jnp.zeros_like(prev_ref)

    rows = lax.broadcasted_iota(jnp.int32, (tm, 1), 0)
    p1 = prev_ref[7:8, :]
    p2 = prev_ref[6:7, :]
    u1 = jnp.where(rows == 0, p1, pltpu.roll(u, 1, axis=0))
    u2 = jnp.where(rows == 0, p2, jnp.where(rows == 1, p1, pltpu.roll(u, 2, axis=0)))
    prev_ref[...] = u[tm - 8:tm, :]
    y = proj[:, 0:c] * (cw_ref[0:1, :] * u2 + cw_ref[1:2, :] * u1 + cw_ref[2:3, :] * u)
    ms = _split_dot(y * y, gmat_ref[...])
    yc_ref[0] = (y * lax.rsqrt(ms + EPS) * gc_ref[...]).astype(BF16)
    o = 3 * c
    q_ref[0] = (proj[:, o:o + SB_DIM] * (SB_HEAD_DIM ** -0.5)).astype(BF16)
    k_ref[0] = proj[:, o + SB_DIM:o + 2 * SB_DIM].astype(BF16)
    v_ref[0] = proj[:, o + 2 * SB_DIM:o + 3 * SB_DIM].astype(BF16)


def _inproj(x, g_mix, w_in, conv_w, g_conv, gmat):
    b, s, d = x.shape
    tm = ROW_TILE
    p = w_in.shape[1]
    row = lambda i, j: (i, j, 0)
    fixed = lambda i, j: (0, 0)
    outs = [jax.ShapeDtypeStruct((b, s, CONV_DIM), BF16)] * 4
    return pl.pallas_call(
        _inproj_kernel,
        grid=(b, s // tm),
        in_specs=[pl.BlockSpec((1, tm, d), row),
                  pl.BlockSpec((1, d), fixed),
                  pl.BlockSpec((d, p), fixed),
                  pl.BlockSpec(conv_w.shape, fixed),
                  pl.BlockSpec((1, CONV_DIM), fixed),
                  pl.BlockSpec((CONV_DIM, CONV_DIM), fixed)],
        out_specs=[pl.BlockSpec((1, tm, CONV_DIM), row)] * 4,
        out_shape=outs,
        scratch_shapes=[pltpu.VMEM((8, CONV_DIM), F32)],
        compiler_params=pltpu.CompilerParams(
            dimension_semantics=("arbitrary", "arbitrary"), vmem_limit_bytes=VMEM_LIMIT),
        name="inproj_conv",
    )(x, g_mix, w_in, conv_w, g_conv, gmat)


def _sb_kernel(q_ref, k_ref, v_ref, tri_ref, g_ref, o_ref):
    tq = q_ref.shape[1]
    tk = tq
    qi = pl.program_id(2)
    q = q_ref[0]
    lane = lax.broadcasted_iota(jnp.int32, (1, LANES), 1)
    in_a = lane < SB_HEAD_DIM
    zero = jnp.zeros_like(q)
    q_heads = (jnp.where(in_a, q, zero), jnp.where(in_a, zero, q))
    tri = tri_ref[...]

    def scores(qh, kb):
        return lax.dot_general(qh, kb, (((1,), (1,)), ((), ())), preferred_element_type=F32) * LOG2E

    def later_keys(z2, valid):
        sp2 = jnp.maximum(z2, 0.0) + jnp.log2(1.0 + jnp.exp2(-jnp.abs(z2)))
        if valid is not None:
            sp2 = jnp.where(valid, sp2, 0.0)
        hi = sp2.astype(BF16)
        lo = (sp2 - hi.astype(F32)).astype(BF16)
        rest = jnp.dot(jnp.concatenate([hi, lo], axis=1), tri,
                       preferred_element_type=F32)
        return sp2, rest

    def weights(z2, sp2, rest, run, valid):
        a = jnp.exp2(z2 - sp2 - rest - run)
        if valid is not None:
            a = jnp.where(valid, a, 0.0)
        return a.astype(BF16), run + rest[:, 0:1] + sp2[:, 0:1]

    def tiles(js, carry, valid):
        run = [carry[0], carry[2]]
        acc = [carry[1], carry[3]]
        kbs, vbs = [], []
        for j in js:
            start = pl.multiple_of(j * tk, tk)
            kbs.append(k_ref[0, pl.ds(start, tk), :])
            vbs.append(v_ref[0, pl.ds(start, tk), :])
        zs = [[scores(qh, kb) for qh in q_heads] for kb in kbs]
        mids = [[later_keys(z, valid) for z in zt] for zt in zs]
        for t in range(len(js)):
            for h in range(2):
                w, run[h] = weights(zs[t][h], *mids[t][h], run[h], valid)
                acc[h] = acc[h] + jnp.dot(w, vbs[t], preferred_element_type=F32)
        return run[0], acc[0], run[1], acc[1]

    r0 = jnp.zeros((tq, 1), F32)
    a0 = jnp.zeros((tq, LANES), F32)
    t_idx = lax.broadcasted_iota(jnp.int32, (tq, tk), 0)
    s_idx = lax.broadcasted_iota(jnp.int32, (tq, tk), 1)
    carry = tiles([qi], (r0, a0, r0, a0), s_idx < t_idx)
    odd = qi % 2
    carry = lax.cond(odd == 1, lambda c: tiles([qi - 1], c, None), lambda c: c, carry)
    first = qi - 1 - odd
    carry = lax.fori_loop(0, qi // 2,
                          lambda n, c: tiles([first - 2 * n, first - 2 * n - 1], c, None), carry)
    out = jnp.where(in_a, carry[1], carry[3])
    sq = out * out
    ss_a = jnp.sum(jnp.where(in_a, sq, 0.0), axis=-1, keepdims=True)
    ss_b = jnp.sum(jnp.where(in_a, 0.0, sq), axis=-1, keepdims=True)
    ms = jnp.where(in_a, ss_a, ss_b) * (1.0 / SB_HEAD_DIM)
    o_ref[0] = (out * lax.rsqrt(ms + EPS) * g_ref[...]).astype(BF16)


def _stick_breaking(q, k, v, tri, g_sb):
    b, s, w = q.shape
    tq = SB_TILE
    return pl.pallas_call(
        _sb_kernel,
        grid=(b, w // LANES, s // tq),
        in_specs=[pl.BlockSpec((1, tq, LANES), lambda i, h, j: (i, j, h)),
                  pl.BlockSpec((1, s, LANES), lambda i, h, j: (i, 0, h)),
                  pl.BlockSpec((1, s, LANES), lambda i, h, j: (i, 0, h)),
                  pl.BlockSpec((2 * tq, tq), lambda i, h, j: (0, 0)),
                  pl.BlockSpec((1, LANES), lambda i, h, j: (0, h))],
        out_specs=pl.BlockSpec((1, tq, LANES), lambda i, h, j: (i, j, h)),
        out_shape=jax.ShapeDtypeStruct((b, s, w), BF16),
        compiler_params=pltpu.CompilerParams(
            dimension_semantics=("arbitrary", "arbitrary", "arbitrary"),
            vmem_limit_bytes=VMEM_LIMIT),
        name="stick_breaking",
    )(q, k, v, tri, g_sb)


def _mid_kernel(x_ref, yc_ref, ys_ref, wo1_ref, wo2_ref, gx_ref, wq_ref, kv_ref, wom_ref,
                gm_ref, wr_ref, br_ref, ltri_ref,
                x2_ref, h3_ref, route_ref, cnt_ref, carry_ref):
    tm = x_ref.shape[1]
    d = x_ref.shape[2]
    hd = d // X_HEADS

    @pl.when((pl.program_id(0) == 0) & (pl.program_id(1) == 0))
    def _():
        carry_ref[...] = jnp.zeros_like(carry_ref)

    x1 = (x_ref[0]
          + jnp.dot(yc_ref[0], wo1_ref[...], preferred_element_type=F32)
          + jnp.dot(ys_ref[0], wo2_ref[...], preferred_element_type=F32))
    h2 = _rms(x1, gx_ref[...]).astype(BF16)
    qm = (jnp.dot(h2, wq_ref[...], preferred_element_type=F32) * (hd ** -0.5)).astype(BF16)
    heads = []
    for h in range(X_HEADS):
        kh = kv_ref[0, :, h * hd:(h + 1) * hd]
        vh = kv_ref[0, :, d + h * hd:d + (h + 1) * hd]
        sc = lax.dot_general(qm[:, h * hd:(h + 1) * hd], kh, (((1,), (1,)), ((), ())),
                             preferred_element_type=F32)
        e = jnp.exp(sc - jnp.max(sc, axis=-1, keepdims=True))
        p = e / jnp.sum(e, axis=-1, keepdims=True)
        heads.append(jnp.dot(p.astype(BF16), vh, preferred_element_type=F32).astype(BF16))
    x2 = x1 + jnp.dot(jnp.concatenate(heads, axis=-1), wom_ref[...], preferred_element_type=F32)
    x2_ref[0] = x2
    h3 = _rms(x2, gm_ref[...])
    h3_ref[0] = h3

    h_hi = h3.astype(BF16)
    h_lo = (h3 - h_hi.astype(F32)).astype(BF16)
    logits = jnp.dot(jnp.concatenate([h_hi, h_lo, h_hi], axis=1), wr_ref[...],
                     preferred_element_type=F32) + br_ref[...]
    lane = lax.broadcasted_iota(jnp.int32, (tm, N_EXPERTS), 1).astype(F32)
    vals, idxs = [], []
    l = logits
    for _ in range(TOP_K):
        m = jnp.max(l, axis=-1, keepdims=True)
        idx = jnp.min(jnp.where(l == m, lane, float(N_EXPERTS)), axis=-1, keepdims=True)
        vals.append(m)
        idxs.append(idx)
        l = jnp.where(lane == idx, -jnp.inf, l)
    es = [jnp.exp(vk - vals[0]) for vk in vals]
    den = es[0] + es[1] + es[2] + es[3]
    gates = [ek / den for ek in es]

    sel = jnp.zeros((tm, N_EXPERTS), F32)
    for idx in idxs:
        sel = sel + (lane == idx).astype(F32)
    before = jnp.dot(ltri_ref[...], sel.astype(BF16), preferred_element_type=F32) + carry_ref[...]
    ranks = [jnp.sum(jnp.where(lane == idx, before, 0.0), axis=-1, keepdims=True) for idx in idxs]
    carry_ref[...] += jnp.sum(sel, axis=0, keepdims=True)
    cnt_ref[...] = carry_ref[...]

    lane_o = lax.broadcasted_iota(jnp.int32, (tm, LANES), 1)
    r = jnp.zeros((tm, LANES), F32)
    for k in range(TOP_K):
        r = jnp.where(lane_o == k, idxs[k], r)
        r = jnp.where(lane_o == TOP_K + k, gates[k], r)
        r = jnp.where(lane_o == 2 * TOP_K + k, ranks[k], r)
    route_ref[0] = r


def _mid(x, yc, ys, wo1, wo2, g_x, w_q, kv, w_om, g_moe, w_r, b_r, ltri):
    b, s, d = x.shape
    tm = ROW_TILE
    m = kv.shape[1]
    row = lambda i, j: (i, j, 0)
    fixed = lambda i, j: (0, 0)
    return pl.pallas_call(
        _mid_kernel,
        grid=(b, s // tm),
        in_specs=[pl.BlockSpec((1, tm, d), row),
                  pl.BlockSpec((1, tm, CONV_DIM), row),
                  pl.BlockSpec((1, tm, SB_DIM), row),
                  pl.BlockSpec((CONV_DIM, d), fixed),
                  pl.BlockSpec((SB_DIM, d), fixed),
                  pl.BlockSpec((1, d), fixed),
                  pl.BlockSpec((d, d), fixed),
                  pl.BlockSpec((1, m, 2 * d), lambda i, j: (i, 0, 0)),
                  pl.BlockSpec((d, d), fixed),
                  pl.BlockSpec((1, d), fixed),
                  pl.BlockSpec((3 * d, N_EXPERTS), fixed),
                  pl.BlockSpec((1, N_EXPERTS), fixed),
                  pl.BlockSpec((tm, tm), fixed)],
        out_specs=[pl.BlockSpec((1, tm, d), row),
                   pl.BlockSpec((1, tm, d), row),
                   pl.BlockSpec((1, tm, LANES), row),
                   pl.BlockSpec((1, N_EXPERTS), fixed)],
        out_shape=[jax.ShapeDtypeStruct((b, s, d), F32),
                   jax.ShapeDtypeStruct((b, s, d), F32),
                   jax.ShapeDtypeStruct((b, s, LANES), F32),
                   jax.ShapeDtypeStruct((1, N_EXPERTS), F32)],
        scratch_shapes=[pltpu.VMEM((1, N_EXPERTS), F32)],
        compiler_params=pltpu.CompilerParams(
            dimension_semantics=("arbitrary", "arbitrary"), vmem_limit_bytes=VMEM_LIMIT),
        name="outproj_xattn_router",
    )(x, yc, ys, wo1, wo2, g_x, w_q, kv, w_om, g_moe, w_r, b_r, ltri)


def _row_copy(src, src_row, dst, dst_row, sem):
    return pltpu.make_async_copy(src.at[pl.ds(src_row, 1), :], dst.at[pl.ds(dst_row, 1), :], sem)


def _dispatch_kernel(cnt_ref, pad_ref, start_ref, dest_ref, h_ref, zero_ref, xs_ref, stage_ref, sem, zsem):
    i = pl.program_id(0)
    n_steps = pl.num_programs(0)
    tm = h_ref.shape[0]
    te = zero_ref.shape[0]
    slot = i % 2

    def block_copy(row):
        rows = pl.ds(pl.multiple_of(row, te), te)
        return pltpu.make_async_copy(zero_ref, xs_ref.at[rows, :], zsem)

    @pl.when(i == 0)
    def _():
        last = N_EXPERTS - 1
        n_used = (start_ref[last] + pad_ref[last]) // te
        n_blocks = xs_ref.shape[0] // te

        def fill(wait):
            def expert(e, c):
                @pl.when(pad_ref[e] > cnt_ref[e])
                def _():
                    cp = block_copy(start_ref[e] + pad_ref[e] - te)
                    cp.wait() if wait else cp.start()
                return c

            def tail(j, c):
                cp = block_copy(j * te)
                cp.wait() if wait else cp.start()
                return c

            lax.fori_loop(0, N_EXPERTS, expert, 0)
            lax.fori_loop(n_used, n_blocks, tail, 0)

        fill(False)
        fill(True)

    def drain(s):
        def body(t, c):
            for k in range(TOP_K):
                _row_copy(stage_ref.at[s], t, xs_ref, 0, sem.at[s]).wait()
            return c
        lax.fori_loop(0, tm, body, 0)

    @pl.when(i >= 2)
    def _():
        drain(slot)

    stage_ref[slot] = h_ref[...]

    def issue(t, c):
        for k in range(TOP_K):
            _row_copy(stage_ref.at[slot], t, xs_ref, dest_ref[t * TOP_K + k],
                      sem.at[slot]).start(priority=k % 2)
        return c

    lax.fori_loop(0, tm, issue, 0, unroll=8)

    @pl.when(i == n_steps - 1)
    def _():
        @pl.when(n_steps > 1)
        def _():
            drain(1 - slot)
        drain(slot)


def _dispatch(counts, padded, start, dest_flat, h3, n_rows):
    n, d = h3.shape
    tm = ROW_TILE
    zero = jnp.zeros((EXPERT_TILE, d), F32)
    return pl.pallas_call(
        _dispatch_kernel,
        grid_spec=pltpu.PrefetchScalarGridSpec(
            num_scalar_prefetch=3,
            grid=(n // tm,),
            in_specs=[pl.BlockSpec((tm * TOP_K,), lambda i, *_: (i,), memory_space=pltpu.SMEM),
                      pl.BlockSpec((tm, d), lambda i, *_: (i, 0)),
                      pl.BlockSpec(zero.shape, lambda i, *_: (0, 0))],
            out_specs=pl.BlockSpec(memory_space=pl.ANY),
            scratch_shapes=[pltpu.VMEM((2, tm, d), F32), pltpu.SemaphoreType.DMA((2,)),
                            pltpu.SemaphoreType.DMA]),
        out_shape=jax.ShapeDtypeStruct((n_rows, d), F32),
        compiler_params=pltpu.CompilerParams(
            dimension_semantics=("arbitrary",), vmem_limit_bytes=VMEM_LIMIT),
        name="moe_dispatch",
    )(counts, padded, start, dest_flat, h3, zero)


def _expert_kernel(be_ref, nu_ref, xs_ref, w1_ref, b1_ref, w2_ref, b2_ref, o_ref, w1c_ref, w2c_ref):
    i = pl.program_id(0)
    de = w2_ref.shape[1]

    @pl.when(i >= nu_ref[0])
    def _():
        o_ref[...] = jnp.zeros_like(o_ref)

    @pl.when(i < nu_ref[0])
    def _():
        changed = (i == 0) | (be_ref[i] != be_ref[jnp.maximum(i - 1, 0)])

        @pl.when(changed)
        def _():
            w1c_ref[...] = w1_ref[0].astype(BF16)
            w2c_ref[...] = w2_ref[0].astype(BF16)

        xb = xs_ref[...].astype(BF16)
        ch = EXPERT_CHUNK

        def activation(c):
            g = (jnp.dot(xb, w1c_ref[:, c * ch:(c + 1) * ch], preferred_element_type=F32)
                 + b1_ref[0, :, c * ch:(c + 1) * ch])
            l = (jnp.dot(xb, w1c_ref[:, de + c * ch:de + (c + 1) * ch], preferred_element_type=F32)
                 + b1_ref[0, :, de + c * ch:de + (c + 1) * ch])
            glu = jnp.minimum(g, SWIGLU_LIMIT)
            lin = jnp.clip(l, -SWIGLU_LIMIT, SWIGLU_LIMIT)
            half = 0.5 * glu
            return ((half + half * jnp.tanh((0.5 * SWIGLU_ALPHA) * glu)) * (lin + 1.0)).astype(BF16)

        out = b2_ref[0]
        act = activation(0)
        for c in range(de // ch):
            nxt = activation(c + 1) if (c + 1) * ch < de else None
            out = out + jnp.dot(act, w2c_ref[c * ch:(c + 1) * ch, :], preferred_element_type=F32)
            act = nxt
        o_ref[...] = out


def _experts(block_e, n_used, xs, w1, b1, w2, b2):
    n_rows, d = xs.shape
    tm = EXPERT_TILE
    ne, _, dh = w1.shape
    de = w2.shape[1]
    rows = lambda i, be, nu: (jnp.maximum(jnp.minimum(i, nu[0] - 1), 0), 0)
    per_e = lambda i, be, nu: (be[i], 0, 0)
    return pl.pallas_call(
        _expert_kernel,
        grid_spec=pltpu.PrefetchScalarGridSpec(
            num_scalar_prefetch=2,
            grid=(n_rows // tm,),
            in_specs=[pl.BlockSpec((tm, d), rows),
                      pl.BlockSpec((1, d, dh), per_e),
                      pl.BlockSpec((1, 1, dh), per_e),
                      pl.BlockSpec((1, de, d), per_e),
                      pl.BlockSpec((1, 1, d), per_e)],
            out_specs=pl.BlockSpec((tm, d), lambda i, be, nu: (i, 0)),
            scratch_shapes=[pltpu.VMEM((d, dh), BF16), pltpu.VMEM((de, d), BF16)]),
        out_shape=jax.ShapeDtypeStruct(xs.shape, F32),
        compiler_params=pltpu.CompilerParams(
            dimension_semantics=("arbitrary",), vmem_limit_bytes=VMEM_LIMIT),
        name="moe_experts",
    )(block_e, n_used, xs, w1, b1.reshape(ne, 1, dh), w2, b2.reshape(ne, 1, d))


def _combine_kernel(dest_ref, next_dest_ref, x2_ref, route_ref, g_ref, ys_ref, o_ref, buf_ref, sem):
    i = pl.program_id(0)
    n_steps = pl.num_programs(0)
    tm = x2_ref.shape[0]
    slot = i % 2

    def gather(idx_ref, s, wait):
        def body(t, c):
            for k in range(TOP_K):
                cp = _row_copy(ys_ref, idx_ref[t * TOP_K + k], buf_ref.at[s, k], t, sem.at[s])
                cp.wait() if wait else cp.start(priority=k % 2)
            return c
        if wait:
            lax.fori_loop(0, tm, body, 0)
        else:
            lax.fori_loop(0, tm, body, 0, unroll=8)

    @pl.when(i == 0)
    def _():
        gather(dest_ref, slot, False)

    @pl.when(i + 1 < n_steps)
    def _():
        gather(next_dest_ref, 1 - slot, False)

    gather(dest_ref, slot, True)
    route = route_ref[...]
    x3 = x2_ref[...]
    for k in range(TOP_K):
        x3 = x3 + buf_ref[slot, k] * route[:, TOP_K + k:TOP_K + k + 1]
    o_ref[...] = _rms(x3, g_ref[...])


def _combine(dest_flat, x2, route, g_final, ys):
    n, d = x2.shape
    tm = ROW_TILE
    return pl.pallas_call(
        _combine_kernel,
        grid=(n // tm,),
        in_specs=[pl.BlockSpec((tm * TOP_K,), lambda i: (i,), memory_space=pltpu.SMEM),
                  pl.BlockSpec((tm * TOP_K,), lambda i: (jnp.minimum(i + 1, n // tm - 1),),
                               memory_space=pltpu.SMEM),
                  pl.BlockSpec((tm, d), lambda i: (i, 0)),
                  pl.BlockSpec((tm, LANES), lambda i: (i, 0)),
                  pl.BlockSpec((1, d), lambda i: (0, 0)),
                  pl.BlockSpec(memory_space=pl.ANY)],
        out_specs=pl.BlockSpec((tm, d), lambda i: (i, 0)),
        out_shape=jax.ShapeDtypeStruct((n, d), F32),
        scratch_shapes=[pltpu.VMEM((2, TOP_K, tm, d), F32), pltpu.SemaphoreType.DMA((2,))],
        compiler_params=pltpu.CompilerParams(
            dimension_semantics=("arbitrary",), vmem_limit_bytes=VMEM_LIMIT),
        name="moe_combine_norm",
    )(dest_flat, dest_flat, x2, route, g_final, ys)


def _layer(x, mem, g_mix, w_in, conv_w, g_conv_out, g_sb_out, w_out, g_xattn, g_mem,
           w_q_mem, w_kv_mem, w_o_mem, g_moe, w_router, b_router, w1, b1, w2, b2):
    b, s, d = x.shape
    n = b * s
    row2 = lambda g: g.reshape(1, -1)

    grp = jnp.arange(CONV_DIM) // CONV_GROUP_DIM
    gmat = jnp.where(grp[:, None] == grp[None, :], 1.0 / CONV_GROUP_DIM, 0.0).astype(BF16)
    ar = jnp.arange(SB_TILE)
    tri = (ar[:, None] > ar[None, :]).astype(BF16)
    tri = jnp.concatenate([tri, tri], axis=0)
    ar = jnp.arange(ROW_TILE)
    ltri = (ar[:, None] > ar[None, :]).astype(BF16)

    kv = _mem_kv(mem, row2(g_mem), w_kv_mem.astype(BF16))
    yc, q, k, v = _inproj(x, row2(g_mix), w_in.astype(BF16), conv_w, row2(g_conv_out), gmat)
    ys = _stick_breaking(q, k, v, tri, row2(g_sb_out))
    w_out_b = w_out.astype(BF16)
    wr_hi = w_router.astype(BF16)
    wr_lo = (w_router - wr_hi.astype(F32)).astype(BF16)
    x2, h3, route, counts = _mid(x, yc, ys, w_out_b[:CONV_DIM], w_out_b[CONV_DIM:], row2(g_xattn),
                                 w_q_mem.astype(BF16), kv, w_o_mem.astype(BF16), row2(g_moe),
                                 jnp.concatenate([wr_hi, wr_hi, wr_lo], axis=0), row2(b_router), ltri)

    te = EXPERT_TILE
    route = route.reshape(n, LANES)
    counts = counts.reshape(N_EXPERTS).astype(jnp.int32)
    padded = (counts + te - 1) // te * te
    padded_end = jnp.cumsum(padded)
    start = padded_end - padded
    top_i = route[:, 0:TOP_K].astype(jnp.int32)
    rank = route[:, 2 * TOP_K:3 * TOP_K].astype(jnp.int32)
    experts = jnp.arange(N_EXPERTS, dtype=jnp.int32)
    first = jnp.sum(jnp.where(top_i[:, :, None] == experts, start, 0), axis=-1)
    dest = (first + rank).reshape(n * TOP_K)
    n_rows = n * TOP_K + N_EXPERTS * te
    n_blocks = n_rows // te
    n_used = (padded_end[-1] // te).astype(jnp.int32)
    blk = jnp.minimum(jnp.arange(n_blocks, dtype=jnp.int32), n_used - 1) * te
    block_e = jnp.minimum(jnp.sum((padded_end[None, :] <= blk[:, None]).astype(jnp.int32), axis=1),
                          N_EXPERTS - 1)

    xs = _dispatch(counts, padded, start, dest, h3.reshape(n, d), n_rows)
    yo = _experts(block_e, n_used.reshape(1), xs, w1, b1, w2, b2)
    return dest, x2.reshape(n, d), route, yo


def kernel(x, mem, g_mix, w_in, conv_w, g_conv_out, g_sb_out, w_out, g_xattn, g_mem, w_q_mem,
           w_kv_mem, w_o_mem, g_moe, w_router, b_router, w1, b1, w2, b2, g_final):
    b, s, d = x.shape
    assert w_in.shape[0] == 1, "single-layer stack"
    dest, x2, route, yo = _layer(x, mem, g_mix[0], w_in[0], conv_w[0], g_conv_out[0], g_sb_out[0],
                                 w_out[0], g_xattn[0], g_mem[0], w_q_mem[0], w_kv_mem[0],
                                 w_o_mem[0], g_moe[0], w_router[0], b_router[0],
                                 w1[0], b1[0], w2[0], b2[0])
    out = _combine(dest, x2, route, g_final.reshape(1, d), yo)
    return out.reshape(b, s, d)
```

```python
import jax
import jax.numpy as jnp
from jax import lax
from jax.experimental import pallas as pl
from jax.experimental.pallas import tpu as pltpu

F32 = jnp.float32
BF16 = jnp.bfloat16

EPS = 1e-5
LOG2E = 1.4426950408889634
CONV_DIM = 512
SB_DIM = 512
SB_HEAD_DIM = 64
CONV_GROUP_DIM = 64
X_HEADS = 4
N_EXPERTS = 32
TOP_K = 4
SWIGLU_LIMIT = 7.0
SWIGLU_ALPHA = 1.702

LANES = 128
ROW_TILE = 512
SB_TILE = 256
EXPERT_TILE = 512
EXPERT_CHUNK = 256
VMEM_LIMIT = 56 * 1024 * 1024


def _rms(x, g):
    return x * lax.rsqrt(jnp.mean(x * x, axis=-1, keepdims=True) + EPS) * g


def _split_dot(a, m):
    hi = a.astype(BF16)
    lo = (a - hi.astype(F32)).astype(BF16)
    return (jnp.dot(hi, m, preferred_element_type=F32)
            + jnp.dot(lo, m, preferred_element_type=F32))


def _mem_kv_kernel(mem_ref, g_ref, w_ref, kv_ref):
    h = _rms(mem_ref[0], g_ref[...]).astype(BF16)
    kv_ref[0] = jnp.dot(h, w_ref[...], preferred_element_type=F32).astype(BF16)


def _mem_kv(mem, g_mem, w_kv):
    b, m, d = mem.shape
    return pl.pallas_call(
        _mem_kv_kernel,
        grid=(b,),
        in_specs=[pl.BlockSpec((1, m, d), lambda i: (i, 0, 0)),
                  pl.BlockSpec((1, d), lambda i: (0, 0)),
                  pl.BlockSpec((d, 2 * d), lambda i: (0, 0))],
        out_specs=pl.BlockSpec((1, m, 2 * d), lambda i: (i, 0, 0)),
        out_shape=jax.ShapeDtypeStruct((b, m, 2 * d), BF16),
        compiler_params=pltpu.CompilerParams(
            dimension_semantics=("arbitrary",), vmem_limit_bytes=VMEM_LIMIT),
        name="mem_kv",
    )(mem, g_mem, w_kv)


def _inproj_kernel(x_ref, g_ref, w_ref, cw_ref, gc_ref, gmat_ref,
                   yc_ref, q_ref, k_ref, v_ref, prev_ref):
    tm = x_ref.shape[1]
    h = _rms(x_ref[0], g_ref[...]).astype(BF16)
    proj = jnp.dot(h, w_ref[...], preferred_element_type=F32)
    c = CONV_DIM
    u = proj[:, c:2 * c] * proj[:, 2 * c:3 * c]

    @pl.when(pl.program_id(1) == 0)
    def _():
        prev_ref[...] = jnp.zeros_like(prev_ref)

    rows = lax.broadcasted_iota(jnp.int32, (tm, 1), 0)
    p1 = prev_ref[7:8, :]
    p2 = prev_ref[6:7, :]
    u1 = jnp.where(rows == 0, p1, pltpu.roll(u, 1, axis=0))
    u2 = jnp.where(rows == 0, p2, jnp.where(rows == 1, p1, pltpu.roll(u, 2, axis=0)))
    prev_ref[...] = u[tm - 8:tm, :]
    y = proj[:, 0:c] * (cw_ref[0:1, :] * u2 + cw_ref[1:2, :] * u1 + cw_ref[2:3, :] * u)
    ms = _split_dot(y * y, gmat_ref[...])
    yc_ref[0] = (y * lax.rsqrt(ms + EPS) * gc_ref[...]).astype(BF16)
    o = 3 * c
    q_ref[0] = (proj[:, o:o + SB_DIM] * (SB_HEAD_DIM ** -0.5)).astype(BF16)
    k_ref[0] = proj[:, o + SB_DIM:o + 2 * SB_DIM].astype(BF16)
    v_ref[0] = proj[:, o + 2 * SB_DIM:o + 3 * SB_DIM].astype(BF16)


def _inproj(x, g_mix, w_in, conv_w, g_conv, gmat):
    b, s, d = x.shape
    tm = ROW_TILE
    p = w_in.shape[1]
    row = lambda i, j: (i, j, 0)
    fixed = lambda i, j: (0, 0)
    outs = [jax.ShapeDtypeStruct((b, s, CONV_DIM), BF16)] * 4
    return pl.pallas_call(
        _inproj_kernel,
        grid=(b, s // tm),
        in_specs=[pl.BlockSpec((1, tm, d), row),
                  pl.BlockSpec((1, d), fixed),
                  pl.BlockSpec((d, p), fixed),
                  pl.BlockSpec(conv_w.shape, fixed),
                  pl.BlockSpec((1, CONV_DIM), fixed),
                  pl.BlockSpec((CONV_DIM, CONV_DIM), fixed)],
        out_specs=[pl.BlockSpec((1, tm, CONV_DIM), row)] * 4,
        out_shape=outs,
        scratch_shapes=[pltpu.VMEM((8, CONV_DIM), F32)],
        compiler_params=pltpu.CompilerParams(
            dimension_semantics=("arbitrary", "arbitrary"), vmem_limit_bytes=VMEM_LIMIT),
        name="inproj_conv",
    )(x, g_mix, w_in, conv_w, g_conv, gmat)


def _sb_kernel(q_ref, k_ref, v_ref, tri_ref, g_ref, o_ref):
    tq = q_ref.shape[1]
    tk = tq
    qi = pl.program_id(2)
    q = q_ref[0]
    lane = lax.broadcasted_iota(jnp.int32, (1, LANES), 1)
    in_a = lane < SB_HEAD_DIM
    zero = jnp.zeros_like(q)
    q_heads = (jnp.where(in_a, q, zero), jnp.where(in_a, zero, q))
    tri = tri_ref[...]

    def scores(qh, kb):
        return lax.dot_general(qh, kb, (((1,), (1,)), ((), ())), preferred_element_type=F32) * LOG2E

    def later_keys(z2, valid):
        sp2 = jnp.maximum(z2, 0.0) + jnp.log2(1.0 + jnp.exp2(-jnp.abs(z2)))
        if valid is not None:
            sp2 = jnp.where(valid, sp2, 0.0)
        hi = sp2.astype(BF16)
        lo = (sp2 - hi.astype(F32)).astype(BF16)
        rest = jnp.dot(jnp.concatenate([hi, lo], axis=1), tri,
                       preferred_element_type=F32)
        return sp2, rest

    def weights(z2, sp2, rest, run, valid):
        a = jnp.exp2(z2 - sp2 - rest - run)
        if valid is not None:
            a = jnp.where(valid, a, 0.0)
        return a.astype(BF16), run + rest[:, 0:1] + sp2[:, 0:1]

    def tiles(js, carry, valid):
        run = [carry[0], carry[2]]
        acc = [carry[1], carry[3]]
        kbs, vbs = [], []
        for j in js:
            start = pl.multiple_of(j * tk, tk)
            kbs.append(k_ref[0, pl.ds(start, tk), :])
            vbs.append(v_ref[0, pl.ds(start, tk), :])
        zs = [[scores(qh, kb) for qh in q_heads] for kb in kbs]
        mids = [[later_keys(z, valid) for z in zt] for zt in zs]
        for t in range(len(js)):
            for h in range(2):
                w, run[h] = weights(zs[t][h], *mids[t][h], run[h], valid)
                acc[h] = acc[h] + jnp.dot(w, vbs[t], preferred_element_type=F32)
        return run[0], acc[0], run[1], acc[1]

    r0 = jnp.zeros((tq, 1), F32)
    a0 = jnp.zeros((tq, LANES), F32)
    t_idx = lax.broadcasted_iota(jnp.int32, (tq, tk), 0)
    s_idx = lax.broadcasted_iota(jnp.int32, (tq, tk), 1)
    carry = tiles([qi], (r0, a0, r0, a0), s_idx < t_idx)
    odd = qi % 2
    carry = lax.cond(odd == 1, lambda c: tiles([qi - 1], c, None), lambda c: c, carry)
    first = qi - 1 - odd
    carry = lax.fori_loop(0, qi // 2,
                          lambda n, c: tiles([first - 2 * n, first - 2 * n - 1], c, None), carry)
    out = jnp.where(in_a, carry[1], carry[3])
    sq = out * out
    ss_a = jnp.sum(jnp.where(in_a, sq, 0.0), axis=-1, keepdims=True)
    ss_b = jnp.sum(jnp.where(in_a, 0.0, sq), axis=-1, keepdims=True)
    ms = jnp.where(in_a, ss_a, ss_b) * (1.0 / SB_HEAD_DIM)
    o_ref[0] = (out * lax.rsqrt(ms + EPS) * g_ref[...]).astype(BF16)


def _stick_breaking(q, k, v, tri, g_sb):
    b, s, w = q.shape
    tq = SB_TILE
    return pl.pallas_call(
        _sb_kernel,
        grid=(b, w // LANES, s // tq),
        in_specs=[pl.BlockSpec((1, tq, LANES), lambda i, h, j: (i, j, h)),
                  pl.BlockSpec((1, s, LANES), lambda i, h, j: (i, 0, h)),
                  pl.BlockSpec((1, s, LANES), lambda i, h, j: (i, 0, h)),
                  pl.BlockSpec((2 * tq, tq), lambda i, h, j: (0, 0)),
                  pl.BlockSpec((1, LANES), lambda i, h, j: (0, h))],
        out_specs=pl.BlockSpec((1, tq, LANES), lambda i, h, j: (i, j, h)),
        out_shape=jax.ShapeDtypeStruct((b, s, w), BF16),
        compiler_params=pltpu.CompilerParams(
            dimension_semantics=("arbitrary", "arbitrary", "arbitrary"),
            vmem_limit_bytes=VMEM_LIMIT),
        name="stick_breaking",
    )(q, k, v, tri, g_sb)


def _mid_kernel(x_ref, yc_ref, ys_ref, wo1_ref, wo2_ref, gx_ref, wq_ref, kv_ref, wom_ref,
                gm_ref, wr_ref, br_ref, ltri_ref,
                x2_ref, h3_ref, route_ref, cnt_ref, carry_ref):
    tm = x_ref.shape[1]
    d = x_ref.shape[2]
    hd = d // X_HEADS

    @pl.when((pl.program_id(0) == 0) & (pl.program_id(1) == 0))
    def _():
        carry_ref[...] = jnp.zeros_like(carry_ref)

    x1 = (x_ref[0]
          + jnp.dot(yc_ref[0], wo1_ref[...], preferred_element_type=F32)
          + jnp.dot(ys_ref[0], wo2_ref[...], preferred_element_type=F32))
    h2 = _rms(x1, gx_ref[...]).astype(BF16)
    qm = (jnp.dot(h2, wq_ref[...], preferred_element_type=F32) * (hd ** -0.5)).astype(BF16)
    heads = []
    for h in range(X_HEADS):
        kh = kv_ref[0, :, h * hd:(h + 1) * hd]
        vh = kv_ref[0, :, d + h * hd:d + (h + 1) * hd]
        sc = lax.dot_general(qm[:, h * hd:(h + 1) * hd], kh, (((1,), (1,)), ((), ())),
                             preferred_element_type=F32)
        e = jnp.exp(sc - jnp.max(sc, axis=-1, keepdims=True))
        p = e / jnp.sum(e, axis=-1, keepdims=True)
        heads.append(jnp.dot(p.astype(BF16), vh, preferred_element_type=F32).astype(BF16))
    x2 = x1 + jnp.dot(jnp.concatenate(heads, axis=-1), wom_ref[...], preferred_element_type=F32)
    x2_ref[0] = x2
    h3 = _rms(x2, gm_ref[...])
    h3_ref[0] = h3

    h_hi = h3.astype(BF16)
    h_lo = (h3 - h_hi.astype(F32)).astype(BF16)
    logits = jnp.dot(jnp.concatenate([h_hi, h_lo, h_hi], axis=1), wr_ref[...],
                     preferred_element_type=F32) + br_ref[...]
    lane = lax.broadcasted_iota(jnp.int32, (tm, N_EXPERTS), 1).astype(F32)
    vals, idxs = [], []
    l = logits
    for _ in range(TOP_K):
        m = jnp.max(l, axis=-1, keepdims=True)
        idx = jnp.min(jnp.where(l == m, lane, float(N_EXPERTS)), axis=-1, keepdims=True)
        vals.append(m)
        idxs.append(idx)
        l = jnp.where(lane == idx, -jnp.inf, l)
    es = [jnp.exp(vk - vals[0]) for vk in vals]
    den = es[0] + es[1] + es[2] + es[3]
    gates = [ek / den for ek in es]

    sel = jnp.zeros((tm, N_EXPERTS), F32)
    for idx in idxs:
        sel = sel + (lane == idx).astype(F32)
    before = jnp.dot(ltri_ref[...], sel.astype(BF16), preferred_element_type=F32) + carry_ref[...]
    ranks = [jnp.sum(jnp.where(lane == idx, before, 0.0), axis=-1, keepdims=True) for idx in idxs]
    carry_ref[...] += jnp.sum(sel, axis=0, keepdims=True)
    cnt_ref[...] = carry_ref[...]

    lane_o = lax.broadcasted_iota(jnp.int32, (tm, LANES), 1)
    r = jnp.zeros((tm, LANES), F32)
    for k in range(TOP_K):
        r = jnp.where(lane_o == k, idxs[k], r)
        r = jnp.where(lane_o == TOP_K + k, gates[k], r)
        r = jnp.where(lane_o == 2 * TOP_K + k, ranks[k], r)
    route_ref[0] = r


def _mid(x, yc, ys, wo1, wo2, g_x, w_q, kv, w_om, g_moe, w_r, b_r, ltri):
    b, s, d = x.shape
    tm = ROW_TILE
    m = kv.shape[1]
    row = lambda i, j: (i, j, 0)
    fixed = lambda i, j: (0, 0)
    return pl.pallas_call(
        _mid_kernel,
        grid=(b, s // tm),
        in_specs=[pl.BlockSpec((1, tm, d), row),
                  pl.BlockSpec((1, tm, CONV_DIM), row),
                  pl.BlockSpec((1, tm, SB_DIM), row),
                  pl.BlockSpec((CONV_DIM, d), fixed),
                  pl.BlockSpec((SB_DIM, d), fixed),
                  pl.BlockSpec((1, d), fixed),
                  pl.BlockSpec((d, d), fixed),
                  pl.BlockSpec((1, m, 2 * d), lambda i, j: (i, 0, 0)),
                  pl.BlockSpec((d, d), fixed),
                  pl.BlockSpec((1, d), fixed),
                  pl.BlockSpec((3 * d, N_EXPERTS), fixed),
                  pl.BlockSpec((1, N_EXPERTS), fixed),
                  pl.BlockSpec((tm, tm), fixed)],
        out_specs=[pl.BlockSpec((1, tm, d), row),
                   pl.BlockSpec((1, tm, d), row),
                   pl.BlockSpec((1, tm, LANES), row),
                   pl.BlockSpec((1, N_EXPERTS), fixed)],
        out_shape=[jax.ShapeDtypeStruct((b, s, d), F32),
                   jax.ShapeDtypeStruct((b, s, d), F32),
                   jax.ShapeDtypeStruct((b, s, LANES), F32),
                   jax.ShapeDtypeStruct((1, N_EXPERTS), F32)],
        scratch_shapes=[pltpu.VMEM((1, N_EXPERTS), F32)],
        compiler_params=pltpu.CompilerParams(
            dimension_semantics=("arbitrary", "arbitrary"), vmem_limit_bytes=VMEM_LIMIT),
        name="outproj_xattn_router",
    )(x, yc, ys, wo1, wo2, g_x, w_q, kv, w_om, g_moe, w_r, b_r, ltri)


def _row_copy(src, src_row, dst, dst_row, sem):
    return pltpu.make_async_copy(src.at[pl.ds(src_row, 1), :], dst.at[pl.ds(dst_row, 1), :], sem)


def _dispatch_kernel(cnt_ref, pad_ref, start_ref, dest_ref, h_ref, zero_ref, xs_ref, stage_ref, sem, zsem):
    i = pl.program_id(0)
    n_steps = pl.num_programs(0)
    tm = h_ref.shape[0]
    te = zero_ref.shape[0]
    slot = i % 2

    def block_copy(row):
        rows = pl.ds(pl.multiple_of(row, te), te)
        return pltpu.make_async_copy(zero_ref, xs_ref.at[rows, :], zsem)

    @pl.when(i == 0)
    def _():
        last = N_EXPERTS - 1
        n_used = (start_ref[last] + pad_ref[last]) // te
        n_blocks = xs_ref.shape[0] // te

        def fill(wait):
            def expert(e, c):
                @pl.when(pad_ref[e] > cnt_ref[e])
                def _():
                    cp = block_copy(start_ref[e] + pad_ref[e] - te)
                    cp.wait() if wait else cp.start()
                return c

            def tail(j, c):
                cp = block_copy(j * te)
                cp.wait() if wait else cp.start()
                return c

            lax.fori_loop(0, N_EXPERTS, expert, 0)
            lax.fori_loop(n_used, n_blocks, tail, 0)

        fill(False)
        fill(True)

    def drain(s):
        def body(t, c):
            for k in range(TOP_K):
                _row_copy(stage_ref.at[s], t, xs_ref, 0, sem.at[s]).wait()
            return c
        lax.fori_loop(0, tm, body, 0)

    @pl.when(i >= 2)
    def _():
        drain(slot)

    stage_ref[slot] = h_ref[...]

    def issue(t, c):
        for k in range(TOP_K):
            _row_copy(stage_ref.at[slot], t, xs_ref, dest_ref[t * TOP_K + k],
                      sem.at[slot]).start(priority=k % 2)
        return c

    lax.fori_loop(0, tm, issue, 0, unroll=8)

    @pl.when(i == n_steps - 1)
    def _():
        @pl.when(n_steps > 1)
        def _():
            drain(1 - slot)
        drain(slot)


def _dispatch(counts, padded, start, dest_flat, h3, n_rows):
    n, d = h3.shape
    tm = ROW_TILE
    zero = jnp.zeros((EXPERT_TILE, d), h3.dtype)
    return pl.pallas_call(
        _dispatch_kernel,
        grid_spec=pltpu.PrefetchScalarGridSpec(
            num_scalar_prefetch=3,
            grid=(n // tm,),
            in_specs=[pl.BlockSpec((tm * TOP_K,), lambda i, *_: (i,), memory_space=pltpu.SMEM),
                      pl.BlockSpec((tm, d), lambda i, *_: (i, 0)),
                      pl.BlockSpec(zero.shape, lambda i, *_: (0, 0))],
            out_specs=pl.BlockSpec(memory_space=pl.ANY),
            scratch_shapes=[pltpu.VMEM((2, tm, d), h3.dtype), pltpu.SemaphoreType.DMA((2,)),
                            pltpu.SemaphoreType.DMA]),
        out_shape=jax.ShapeDtypeStruct((n_rows, d), h3.dtype),
        compiler_params=pltpu.CompilerParams(
            dimension_semantics=("arbitrary",), vmem_limit_bytes=VMEM_LIMIT),
        name="moe_dispatch",
    )(counts, padded, start, dest_flat, h3, zero)


def _expert_kernel(be_ref, nu_ref, bv_ref, xs_ref, w1_ref, b1_ref, w2_ref, b2_ref, o_ref, w1c_ref, w2c_ref):
    i = pl.program_id(0)
    tm = xs_ref.shape[0]
    de = w2_ref.shape[1]

    @pl.when(i >= nu_ref[0])
    def _():
        o_ref[...] = jnp.zeros_like(o_ref)

    @pl.when(i < nu_ref[0])
    def _():
        changed = (i == 0) | (be_ref[i] != be_ref[jnp.maximum(i - 1, 0)])

        @pl.when(changed)
        def _():
            w1c_ref[...] = w1_ref[0].astype(BF16)
            w2c_ref[...] = w2_ref[0].astype(BF16)

        ch = EXPERT_CHUNK

        def activation(xb, c):
            g = (jnp.dot(xb, w1c_ref[:, c * ch:(c + 1) * ch], preferred_element_type=F32)
                 + b1_ref[0, :, c * ch:(c + 1) * ch])
            l = (jnp.dot(xb, w1c_ref[:, de + c * ch:de + (c + 1) * ch], preferred_element_type=F32)
                 + b1_ref[0, :, de + c * ch:de + (c + 1) * ch])
            glu = jnp.minimum(g, SWIGLU_LIMIT)
            lin = jnp.clip(l, -SWIGLU_LIMIT, SWIGLU_LIMIT)
            half = 0.5 * glu
            return ((half + half * jnp.tanh((0.5 * SWIGLU_ALPHA) * glu)) * (lin + 1.0)).astype(BF16)

        def mlp(rows):
            xb = xs_ref[0:rows, :].astype(BF16)
            out = b2_ref[0]
            act = activation(xb, 0)
            for c in range(de // ch):
                nxt = activation(xb, c + 1) if (c + 1) * ch < de else None
                out = out + jnp.dot(act, w2c_ref[c * ch:(c + 1) * ch, :], preferred_element_type=F32)
                act = nxt
            o_ref[0:rows, :] = out
            if rows < tm:
                o_ref[rows:tm, :] = jnp.zeros((tm - rows, o_ref.shape[1]), F32)

        @pl.when(bv_ref[i] > tm // 2)
        def _():
            mlp(tm)

        @pl.when(bv_ref[i] <= tm // 2)
        def _():
            mlp(tm // 2)


def _experts(block_e, n_used, block_valid, xs, w1, b1, w2, b2):
    n_rows, dp = xs.shape
    tm = EXPERT_TILE
    ne, d, dh = w1.shape
    de = w2.shape[1]
    rows = lambda i, be, nu, bv: (jnp.maximum(jnp.minimum(i, nu[0] - 1), 0), 0)
    per_e = lambda i, be, nu, bv: (be[i], 0, 0)
    return pl.pallas_call(
        _expert_kernel,
        grid_spec=pltpu.PrefetchScalarGridSpec(
            num_scalar_prefetch=3,
            grid=(n_rows // tm,),
            in_specs=[pl.BlockSpec((tm, dp), rows),
                      pl.BlockSpec((1, d, dh), per_e),
                      pl.BlockSpec((1, 1, dh), per_e),
                      pl.BlockSpec((1, de, d), per_e),
                      pl.BlockSpec((1, 1, d), per_e)],
            out_specs=pl.BlockSpec((tm, d), lambda i, be, nu, bv: (i, 0)),
            scratch_shapes=[pltpu.VMEM((d, dh), BF16), pltpu.VMEM((de, d), BF16)]),
        out_shape=jax.ShapeDtypeStruct((n_rows, d), F32),
        compiler_params=pltpu.CompilerParams(
            dimension_semantics=("arbitrary",), vmem_limit_bytes=VMEM_LIMIT),
        name="moe_experts",
    )(block_e, n_used, block_valid, xs, w1, b1.reshape(ne, 1, dh), w2, b2.reshape(ne, 1, d))


def _combine_kernel(dest_ref, x2_ref, route_ref, g_ref, ys_ref, o_ref, buf_ref, sem):
    tm = x2_ref.shape[0]

    def issue(t, c):
        for k in range(TOP_K):
            _row_copy(ys_ref, dest_ref[t * TOP_K + k], buf_ref.at[k], t, sem).start(priority=k % 2)
        return c

    def drain(t, c):
        for k in range(TOP_K):
            _row_copy(ys_ref, dest_ref[t * TOP_K + k], buf_ref.at[k], t, sem).wait()
        return c

    lax.fori_loop(0, tm, issue, 0, unroll=8)
    lax.fori_loop(0, tm, drain, 0)
    route = route_ref[...]
    x3 = x2_ref[...]
    for k in range(TOP_K):
        x3 = x3 + buf_ref[k] * route[:, TOP_K + k:TOP_K + k + 1]
    o_ref[...] = _rms(x3, g_ref[...])


def _combine(dest_flat, x2, route, g_final, ys):
    n, d = x2.shape
    tm = ROW_TILE
    return pl.pallas_call(
        _combine_kernel,
        grid=(n // tm,),
        in_specs=[pl.BlockSpec((tm * TOP_K,), lambda i: (i,), memory_space=pltpu.SMEM),
                  pl.BlockSpec((tm, d), lambda i: (i, 0)),
                  pl.BlockSpec((tm, LANES), lambda i: (i, 0)),
                  pl.BlockSpec((1, d), lambda i: (0, 0)),
                  pl.BlockSpec(memory_space=pl.ANY)],
        out_specs=pl.BlockSpec((tm, d), lambda i: (i, 0)),
        out_shape=jax.ShapeDtypeStruct((n, d), F32),
        scratch_shapes=[pltpu.VMEM((TOP_K, tm, d), F32), pltpu.SemaphoreType.DMA],
        compiler_params=pltpu.CompilerParams(
            dimension_semantics=("arbitrary",), vmem_limit_bytes=VMEM_LIMIT),
        name="moe_combine_norm",
    )(dest_flat, x2, route, g_final, ys)


def _layer(x, mem, g_mix, w_in, conv_w, g_conv_out, g_sb_out, w_out, g_xattn, g_mem,
           w_q_mem, w_kv_mem, w_o_mem, g_moe, w_router, b_router, w1, b1, w2, b2):
    b, s, d = x.shape
    n = b * s
    row2 = lambda g: g.reshape(1, -1)

    grp = jnp.arange(CONV_DIM) // CONV_GROUP_DIM
    gmat = jnp.where(grp[:, None] == grp[None, :], 1.0 / CONV_GROUP_DIM, 0.0).astype(BF16)
    ar = jnp.arange(SB_TILE)
    tri = (ar[:, None] > ar[None, :]).astype(BF16)
    tri = jnp.concatenate([tri, tri], axis=0)
    ar = jnp.arange(ROW_TILE)
    ltri = (ar[:, None] > ar[None, :]).astype(BF16)

    kv = _mem_kv(mem, row2(g_mem), w_kv_mem.astype(BF16))
    yc, q, k, v = _inproj(x, row2(g_mix), w_in.astype(BF16), conv_w, row2(g_conv_out), gmat)
    ys = _stick_breaking(q, k, v, tri, row2(g_sb_out))
    w_out_b = w_out.astype(BF16)
    wr_hi = w_router.astype(BF16)
    wr_lo = (w_router - wr_hi.astype(F32)).astype(BF16)
    x2, h3, route, counts = _mid(x, yc, ys, w_out_b[:CONV_DIM], w_out_b[CONV_DIM:], row2(g_xattn),
                                 w_q_mem.astype(BF16), kv, w_o_mem.astype(BF16), row2(g_moe),
                                 jnp.concatenate([wr_hi, wr_hi, wr_lo], axis=0), row2(b_router), ltri)

    te = EXPERT_TILE
    route = route.reshape(n, LANES)
    counts = counts.reshape(N_EXPERTS).astype(jnp.int32)
    padded = (counts + te - 1) // te * te
    padded_end = jnp.cumsum(padded)
    start = padded_end - padded
    top_i = route[:, 0:TOP_K].astype(jnp.int32)
    rank = route[:, 2 * TOP_K:3 * TOP_K].astype(jnp.int32)
    experts = jnp.arange(N_EXPERTS, dtype=jnp.int32)
    first = jnp.sum(jnp.where(top_i[:, :, None] == experts, start, 0), axis=-1)
    dest = (first + rank).reshape(n * TOP_K)
    n_rows = n * TOP_K + N_EXPERTS * te
    n_blocks = n_rows // te
    n_used = (padded_end[-1] // te).astype(jnp.int32)
    blk = jnp.minimum(jnp.arange(n_blocks, dtype=jnp.int32), n_used - 1) * te
    block_e = jnp.minimum(jnp.sum((padded_end[None, :] <= blk[:, None]).astype(jnp.int32), axis=1),
                          N_EXPERTS - 1)

    xs = _dispatch(counts, padded, start, dest, h3.reshape(n, d), n_rows)
    block_valid = jnp.sum(jnp.where(block_e[:, None] == experts, counts - (blk[:, None] - start), 0), axis=1)
    yo = _experts(block_e, n_used.reshape(1), block_valid.astype(jnp.int32), xs, w1, b1, w2, b2)
    return dest, x2.reshape(n, d), route, yo


def kernel(x, mem, g_mix, w_in, conv_w, g_conv_out, g_sb_out, w_out, g_xattn, g_mem, w_q_mem,
           w_kv_mem, w_o_mem, g_moe, w_router, b_router, w1, b1, w2, b2, g_final):
    b, s, d = x.shape
    assert w_in.shape[0] == 1, "single-layer stack"
    dest, x2, route, yo = _layer(x, mem, g_mix[0], w_in[0], conv_w[0], g_conv_out[0], g_sb_out[0],
                                 w_out[0], g_xattn[0], g_mem[0], w_q_mem[0], w_kv_mem[0],
                                 w_o_mem[0], g_moe[0], w_router[0], b_router[0],
                                 w1[0], b1[0], w2[0], b2[0])
    out = _combine(dest, x2, route, g_final.reshape(1, d), yo)
    return out.reshape(b, s, d)
```

```python
import jax
import jax.numpy as jnp
from jax import lax
from jax.experimental import pallas as pl
from jax.experimental.pallas import tpu as pltpu

F32 = jnp.float32
BF16 = jnp.bfloat16

EPS = 1e-5
LOG2E = 1.4426950408889634
CONV_DIM = 512
SB_DIM = 512
SB_HEAD_DIM = 64
CONV_GROUP_DIM = 64
X_HEADS = 4
N_EXPERTS = 32
TOP_K = 4
SWIGLU_LIMIT = 7.0
SWIGLU_ALPHA = 1.702

LANES = 128
ROW_TILE = 512
SB_TILE = 256
EXPERT_TILE = 512
EXPERT_CHUNK = 256
VMEM_LIMIT = 56 * 1024 * 1024


def _rms(x, g):
    return x * lax.rsqrt(jnp.mean(x * x, axis=-1, keepdims=True) + EPS) * g


def _split_dot(a, m):
    hi = a.astype(BF16)
    lo = (a - hi.astype(F32)).astype(BF16)
    return (jnp.dot(hi, m, preferred_element_type=F32)
            + jnp.dot(lo, m, preferred_element_type=F32))


def _mem_kv_kernel(mem_ref, g_ref, w_ref, kv_ref):
    h = _rms(mem_ref[0], g_ref[...]).astype(BF16)
    kv_ref[0] = jnp.dot(h, w_ref[...], preferred_element_type=F32).astype(BF16)


def _mem_kv(mem, g_mem, w_kv):
    b, m, d = mem.shape
    return pl.pallas_call(
        _mem_kv_kernel,
        grid=(b,),
        in_specs=[pl.BlockSpec((1, m, d), lambda i: (i, 0, 0)),
                  pl.BlockSpec((1, d), lambda i: (0, 0)),
                  pl.BlockSpec((d, 2 * d), lambda i: (0, 0))],
        out_specs=pl.BlockSpec((1, m, 2 * d), lambda i: (i, 0, 0)),
        out_shape=jax.ShapeDtypeStruct((b, m, 2 * d), BF16),
        compiler_params=pltpu.CompilerParams(
            dimension_semantics=("arbitrary",), vmem_limit_bytes=VMEM_LIMIT),
        name="mem_kv",
    )(mem, g_mem, w_kv)


def _inproj_kernel(x_ref, g_ref, w_ref, cw_ref, gc_ref, gmat_ref,
                   yc_ref, q_ref, k_ref, v_ref, prev_ref):
    tm = x_ref.shape[1]
    h = _rms(x_ref[0], g_ref[...]).astype(BF16)
    proj = jnp.dot(h, w_ref[...], preferred_element_type=F32)
    c = CONV_DIM
    u = proj[:, c:2 * c] * proj[:, 2 * c:3 * c]

    @pl.when(pl.program_id(1) == 0)
    def _():
        prev_ref[...] = jnp.zeros_like(prev_ref)

    rows = lax.broadcasted_iota(jnp.int32, (tm, 1), 0)
    p1 = prev_ref[7:8, :]
    p2 = prev_ref[6:7, :]
    u1 = jnp.where(rows == 0, p1, pltpu.roll(u, 1, axis=0))
    u2 = jnp.where(rows == 0, p2, jnp.where(rows == 1, p1, pltpu.roll(u, 2, axis=0)))
    prev_ref[...] = u[tm - 8:tm, :]
    y = proj[:, 0:c] * (cw_ref[0:1, :] * u2 + cw_ref[1:2, :] * u1 + cw_ref[2:3, :] * u)
    ms = _split_dot(y * y, gmat_ref[...])
    yc_ref[0] = (y * lax.rsqrt(ms + EPS) * gc_ref[...]).astype(BF16)
    o = 3 * c
    q_ref[0] = (proj[:, o:o + SB_DIM] * (SB_HEAD_DIM ** -0.5)).astype(BF16)
    k_ref[0] = proj[:, o + SB_DIM:o + 2 * SB_DIM].astype(BF16)
    v_ref[0] = proj[:, o + 2 * SB_DIM:o + 3 * SB_DIM].astype(BF16)


def _inproj(x, g_mix, w_in, conv_w, g_conv, gmat):
    b, s, d = x.shape
    tm = ROW_TILE
    p = w_in.shape[1]
    row = lambda i, j: (i, j, 0)
    fixed = lambda i, j: (0, 0)
    outs = [jax.ShapeDtypeStruct((b, s, CONV_DIM), BF16)] * 4
    return pl.pallas_call(
        _inproj_kernel,
        grid=(b, s // tm),
        in_specs=[pl.BlockSpec((1, tm, d), row),
                  pl.BlockSpec((1, d), fixed),
                  pl.BlockSpec((d, p), fixed),
                  pl.BlockSpec(conv_w.shape, fixed),
                  pl.BlockSpec((1, CONV_DIM), fixed),
                  pl.BlockSpec((CONV_DIM, CONV_DIM), fixed)],
        out_specs=[pl.BlockSpec((1, tm, CONV_DIM), row)] * 4,
        out_shape=outs,
        scratch_shapes=[pltpu.VMEM((8, CONV_DIM), F32)],
        compiler_params=pltpu.CompilerParams(
            dimension_semantics=("arbitrary", "arbitrary"), vmem_limit_bytes=VMEM_LIMIT),
        name="inproj_conv",
    )(x, g_mix, w_in, conv_w, g_conv, gmat)


def _sb_kernel(q_ref, k_ref, v_ref, tri_ref, g_ref, o_ref):
    tq = tri_ref.shape[1]
    tk = tq
    lane = lax.broadcasted_iota(jnp.int32, (1, LANES), 1)
    in_a = lane < SB_HEAD_DIM
    tri = tri_ref[...]

    def scores(qh, kb):
        return lax.dot_general(qh, kb, (((1,), (1,)), ((), ())), preferred_element_type=F32) * LOG2E

    def later_keys(z2, valid):
        sp2 = jnp.maximum(z2, 0.0) + jnp.log2(1.0 + jnp.exp2(-jnp.abs(z2)))
        if valid is not None:
            sp2 = jnp.where(valid, sp2, 0.0)
        hi = sp2.astype(BF16)
        lo = (sp2 - hi.astype(F32)).astype(BF16)
        rest = jnp.dot(jnp.concatenate([hi, lo], axis=1), tri,
                       preferred_element_type=F32)
        return sp2, rest

    def weights(z2, sp2, rest, run, valid):
        a = jnp.exp2(z2 - sp2 - rest - run)
        if valid is not None:
            a = jnp.where(valid, a, 0.0)
        return a.astype(BF16), run + rest[:, 0:1] + sp2[:, 0:1]

    def tiles(q_heads, js, carry, valid):
        run = [carry[0], carry[2]]
        acc = [carry[1], carry[3]]
        kbs, vbs = [], []
        for j in js:
            start = pl.multiple_of(j * tk, tk)
            kbs.append(k_ref[0, pl.ds(start, tk), :])
            vbs.append(v_ref[0, pl.ds(start, tk), :])
        zs = [[scores(qh, kb) for qh in q_heads] for kb in kbs]
        mids = [[later_keys(z, valid) for z in zt] for zt in zs]
        for t in range(len(js)):
            for h in range(2):
                w, run[h] = weights(zs[t][h], *mids[t][h], run[h], valid)
                acc[h] = acc[h] + jnp.dot(w, vbs[t], preferred_element_type=F32)
        return run[0], acc[0], run[1], acc[1]

    r0 = jnp.zeros((tq, 1), F32)
    a0 = jnp.zeros((tq, LANES), F32)
    t_idx = lax.broadcasted_iota(jnp.int32, (tq, tk), 0)
    s_idx = lax.broadcasted_iota(jnp.int32, (tq, tk), 1)

    def query_tile(qi, _):
        rows = pl.ds(pl.multiple_of(qi * tq, tq), tq)
        q = q_ref[0, rows, :]
        zero = jnp.zeros_like(q)
        q_heads = (jnp.where(in_a, q, zero), jnp.where(in_a, zero, q))
        carry = tiles(q_heads, [qi], (r0, a0, r0, a0), s_idx < t_idx)
        odd = qi % 2
        carry = lax.cond(odd == 1, lambda c: tiles(q_heads, [qi - 1], c, None), lambda c: c, carry)
        first = qi - 1 - odd
        carry = lax.fori_loop(
            0, qi // 2,
            lambda n, c: tiles(q_heads, [first - 2 * n, first - 2 * n - 1], c, None), carry)
        out = jnp.where(in_a, carry[1], carry[3])
        sq = out * out
        ss_a = jnp.sum(jnp.where(in_a, sq, 0.0), axis=-1, keepdims=True)
        ss_b = jnp.sum(jnp.where(in_a, 0.0, sq), axis=-1, keepdims=True)
        ms = jnp.where(in_a, ss_a, ss_b) * (1.0 / SB_HEAD_DIM)
        o_ref[0, rows, :] = (out * lax.rsqrt(ms + EPS) * g_ref[...]).astype(BF16)
        return 0

    lax.fori_loop(0, q_ref.shape[1] // tq, query_tile, 0)


def _stick_breaking(q, k, v, tri, g_sb):
    b, s, w = q.shape
    tq = SB_TILE
    return pl.pallas_call(
        _sb_kernel,
        grid=(b, w // LANES),
        in_specs=[pl.BlockSpec((1, s, LANES), lambda i, h: (i, 0, h)),
                  pl.BlockSpec((1, s, LANES), lambda i, h: (i, 0, h)),
                  pl.BlockSpec((1, s, LANES), lambda i, h: (i, 0, h)),
                  pl.BlockSpec((2 * tq, tq), lambda i, h: (0, 0)),
                  pl.BlockSpec((1, LANES), lambda i, h: (0, h))],
        out_specs=pl.BlockSpec((1, s, LANES), lambda i, h: (i, 0, h)),
        out_shape=jax.ShapeDtypeStruct((b, s, w), BF16),
        compiler_params=pltpu.CompilerParams(
            dimension_semantics=("arbitrary", "arbitrary"),
            vmem_limit_bytes=VMEM_LIMIT),
        name="stick_breaking",
    )(q, k, v, tri, g_sb)


def _mid_kernel(x_ref, yc_ref, ys_ref, wo1_ref, wo2_ref, gx_ref, wq_ref, kv_ref, wom_ref,
                gm_ref, wr_ref, br_ref, ltri_ref,
                x2_ref, h3_ref, route_ref, cnt_ref, carry_ref):
    tm = x_ref.shape[1]
    d = x_ref.shape[2]
    hd = d // X_HEADS

    @pl.when((pl.program_id(0) == 0) & (pl.program_id(1) == 0))
    def _():
        carry_ref[...] = jnp.zeros_like(carry_ref)

    x1 = (x_ref[0]
          + jnp.dot(yc_ref[0], wo1_ref[...], preferred_element_type=F32)
          + jnp.dot(ys_ref[0], wo2_ref[...], preferred_element_type=F32))
    h2 = _rms(x1, gx_ref[...]).astype(BF16)
    qm = (jnp.dot(h2, wq_ref[...], preferred_element_type=F32) * (hd ** -0.5)).astype(BF16)
    heads = []
    for h in range(X_HEADS):
        kh = kv_ref[0, :, h * hd:(h + 1) * hd]
        vh = kv_ref[0, :, d + h * hd:d + (h + 1) * hd]
        sc = lax.dot_general(qm[:, h * hd:(h + 1) * hd], kh, (((1,), (1,)), ((), ())),
                             preferred_element_type=F32)
        e = jnp.exp(sc - jnp.max(sc, axis=-1, keepdims=True))
        p = e / jnp.sum(e, axis=-1, keepdims=True)
        heads.append(jnp.dot(p.astype(BF16), vh, preferred_element_type=F32).astype(BF16))
    x2 = x1 + jnp.dot(jnp.concatenate(heads, axis=-1), wom_ref[...], preferred_element_type=F32)
    x2_ref[0] = x2
    h3 = _rms(x2, gm_ref[...])
    h3_ref[0] = h3

    h_hi = h3.astype(BF16)
    h_lo = (h3 - h_hi.astype(F32)).astype(BF16)
    logits = jnp.dot(jnp.concatenate([h_hi, h_lo, h_hi], axis=1), wr_ref[...],
                     preferred_element_type=F32) + br_ref[...]
    lane = lax.broadcasted_iota(jnp.int32, (tm, N_EXPERTS), 1).astype(F32)
    vals, idxs = [], []
    l = logits
    for _ in range(TOP_K):
        m = jnp.max(l, axis=-1, keepdims=True)
        idx = jnp.min(jnp.where(l == m, lane, float(N_EXPERTS)), axis=-1, keepdims=True)
        vals.append(m)
        idxs.append(idx)
        l = jnp.where(lane == idx, -jnp.inf, l)
    es = [jnp.exp(vk - vals[0]) for vk in vals]
    den = es[0] + es[1] + es[2] + es[3]
    gates = [ek / den for ek in es]

    sel = jnp.zeros((tm, N_EXPERTS), F32)
    for idx in idxs:
        sel = sel + (lane == idx).astype(F32)
    before = jnp.dot(ltri_ref[...], sel.astype(BF16), preferred_element_type=F32) + carry_ref[...]
    ranks = [jnp.sum(jnp.where(lane == idx, before, 0.0), axis=-1, keepdims=True) for idx in idxs]
    carry_ref[...] += jnp.sum(sel, axis=0, keepdims=True)
    cnt_ref[...] = carry_ref[...]

    lane_o = lax.broadcasted_iota(jnp.int32, (tm, LANES), 1)
    r = jnp.zeros((tm, LANES), F32)
    for k in range(TOP_K):
        r = jnp.where(lane_o == k, idxs[k], r)
        r = jnp.where(lane_o == TOP_K + k, gates[k], r)
        r = jnp.where(lane_o == 2 * TOP_K + k, ranks[k], r)
    route_ref[0] = r


def _mid(x, yc, ys, wo1, wo2, g_x, w_q, kv, w_om, g_moe, w_r, b_r, ltri):
    b, s, d = x.shape
    tm = ROW_TILE
    m = kv.shape[1]
    row = lambda i, j: (i, j, 0)
    fixed = lambda i, j: (0, 0)
    return pl.pallas_call(
        _mid_kernel,
        grid=(b, s // tm),
        in_specs=[pl.BlockSpec((1, tm, d), row),
                  pl.BlockSpec((1, tm, CONV_DIM), row),
                  pl.BlockSpec((1, tm, SB_DIM), row),
                  pl.BlockSpec((CONV_DIM, d), fixed),
                  pl.BlockSpec((SB_DIM, d), fixed),
                  pl.BlockSpec((1, d), fixed),
                  pl.BlockSpec((d, d), fixed),
                  pl.BlockSpec((1, m, 2 * d), lambda i, j: (i, 0, 0)),
                  pl.BlockSpec((d, d), fixed),
                  pl.BlockSpec((1, d), fixed),
                  pl.BlockSpec((3 * d, N_EXPERTS), fixed),
                  pl.BlockSpec((1, N_EXPERTS), fixed),
                  pl.BlockSpec((tm, tm), fixed)],
        out_specs=[pl.BlockSpec((1, tm, d), row),
                   pl.BlockSpec((1, tm, d), row),
                   pl.BlockSpec((1, tm, LANES), row),
                   pl.BlockSpec((1, N_EXPERTS), fixed)],
        out_shape=[jax.ShapeDtypeStruct((b, s, d), F32),
                   jax.ShapeDtypeStruct((b, s, d), F32),
                   jax.ShapeDtypeStruct((b, s, LANES), F32),
                   jax.ShapeDtypeStruct((1, N_EXPERTS), F32)],
        scratch_shapes=[pltpu.VMEM((1, N_EXPERTS), F32)],
        compiler_params=pltpu.CompilerParams(
            dimension_semantics=("arbitrary", "arbitrary"), vmem_limit_bytes=VMEM_LIMIT),
        name="outproj_xattn_router",
    )(x, yc, ys, wo1, wo2, g_x, w_q, kv, w_om, g_moe, w_r, b_r, ltri)


def _row_copy(src, src_row, dst, dst_row, sem):
    return pltpu.make_async_copy(src.at[pl.ds(src_row, 1), :], dst.at[pl.ds(dst_row, 1), :], sem)


def _dispatch_kernel(cnt_ref, pad_ref, start_ref, dest_ref, h_ref, zero_ref, xs_ref, stage_ref, sem, zsem):
    i = pl.program_id(0)
    n_steps = pl.num_programs(0)
    tm = h_ref.shape[0]
    te = zero_ref.shape[0]
    slot = i % 2

    def block_copy(row):
        rows = pl.ds(pl.multiple_of(row, te), te)
        return pltpu.make_async_copy(zero_ref, xs_ref.at[rows, :], zsem)

    @pl.when(i == 0)
    def _():
        last = N_EXPERTS - 1
        n_used = (start_ref[last] + pad_ref[last]) // te
        n_blocks = xs_ref.shape[0] // te

        def fill(wait):
            def expert(e, c):
                @pl.when(pad_ref[e] > cnt_ref[e])
                def _():
                    cp = block_copy(start_ref[e] + pad_ref[e] - te)
                    cp.wait() if wait else cp.start()
                return c

            def tail(j, c):
                cp = block_copy(j * te)
                cp.wait() if wait else cp.start()
                return c

            lax.fori_loop(0, N_EXPERTS, expert, 0)
            lax.fori_loop(n_used, n_blocks, tail, 0)

        fill(False)
        fill(True)

    def drain(s):
        def body(t, c):
            for k in range(TOP_K):
                _row_copy(stage_ref.at[s], t, xs_ref, 0, sem.at[s]).wait()
            return c
        lax.fori_loop(0, tm, body, 0)

    @pl.when(i >= 2)
    def _():
        drain(slot)

    stage_ref[slot] = h_ref[...]

    def issue(t, c):
        for k in range(TOP_K):
            _row_copy(stage_ref.at[slot], t, xs_ref, dest_ref[t * TOP_K + k],
                      sem.at[slot]).start(priority=k % 2)
        return c

    lax.fori_loop(0, tm, issue, 0, unroll=8)

    @pl.when(i == n_steps - 1)
    def _():
        @pl.when(n_steps > 1)
        def _():
            drain(1 - slot)
        drain(slot)


def _dispatch(counts, padded, start, dest_flat, h3, n_rows):
    n, d = h3.shape
    tm = ROW_TILE
    zero = jnp.zeros((EXPERT_TILE, d), h3.dtype)
    return pl.pallas_call(
        _dispatch_kernel,
        grid_spec=pltpu.PrefetchScalarGridSpec(
            num_scalar_prefetch=3,
            grid=(n // tm,),
            in_specs=[pl.BlockSpec((tm * TOP_K,), lambda i, *_: (i,), memory_space=pltpu.SMEM),
                      pl.BlockSpec((tm, d), lambda i, *_: (i, 0)),
                      pl.BlockSpec(zero.shape, lambda i, *_: (0, 0))],
            out_specs=pl.BlockSpec(memory_space=pl.ANY),
            scratch_shapes=[pltpu.VMEM((2, tm, d), h3.dtype), pltpu.SemaphoreType.DMA((2,)),
                            pltpu.SemaphoreType.DMA]),
        out_shape=jax.ShapeDtypeStruct((n_rows, d), h3.dtype),
        compiler_params=pltpu.CompilerParams(
            dimension_semantics=("arbitrary",), vmem_limit_bytes=VMEM_LIMIT),
        name="moe_dispatch",
    )(counts, padded, start, dest_flat, h3, zero)


def _expert_kernel(be_ref, nu_ref, bv_ref, xs_ref, w1_ref, b1_ref, w2_ref, b2_ref, o_ref, w1c_ref, w2c_ref):
    i = pl.program_id(0)
    tm = xs_ref.shape[0]
    de = w2_ref.shape[1]

    @pl.when(i >= nu_ref[0])
    def _():
        o_ref[...] = jnp.zeros_like(o_ref)

    @pl.when(i < nu_ref[0])
    def _():
        changed = (i == 0) | (be_ref[i] != be_ref[jnp.maximum(i - 1, 0)])

        @pl.when(changed)
        def _():
            w1c_ref[...] = w1_ref[0].astype(BF16)
            w2c_ref[...] = w2_ref[0].astype(BF16)

        ch = EXPERT_CHUNK

        def activation(xb, c):
            g = (jnp.dot(xb, w1c_ref[:, c * ch:(c + 1) * ch], preferred_element_type=F32)
                 + b1_ref[0, :, c * ch:(c + 1) * ch])
            l = (jnp.dot(xb, w1c_ref[:, de + c * ch:de + (c + 1) * ch], preferred_element_type=F32)
                 + b1_ref[0, :, de + c * ch:de + (c + 1) * ch])
            glu = jnp.minimum(g, SWIGLU_LIMIT)
            lin = jnp.clip(l, -SWIGLU_LIMIT, SWIGLU_LIMIT)
            half = 0.5 * glu
            return ((half + half * jnp.tanh((0.5 * SWIGLU_ALPHA) * glu)) * (lin + 1.0)).astype(BF16)

        def mlp(rows):
            xb = xs_ref[0:rows, :].astype(BF16)
            out = b2_ref[0]
            act = activation(xb, 0)
            for c in range(de // ch):
                nxt = activation(xb, c + 1) if (c + 1) * ch < de else None
                out = out + jnp.dot(act, w2c_ref[c * ch:(c + 1) * ch, :], preferred_element_type=F32)
                act = nxt
            o_ref[0:rows, :] = out
            if rows < tm:
                o_ref[rows:tm, :] = jnp.zeros((tm - rows, o_ref.shape[1]), F32)

        @pl.when(bv_ref[i] > tm // 2)
        def _():
            mlp(tm)

        @pl.when(bv_ref[i] <= tm // 2)
        def _():
            mlp(tm // 2)


def _experts(block_e, n_used, block_valid, xs, w1, b1, w2, b2):
    n_rows, dp = xs.shape
    tm = EXPERT_TILE
    ne, d, dh = w1.shape
    de = w2.shape[1]
    rows = lambda i, be, nu, bv: (jnp.maximum(jnp.minimum(i, nu[0] - 1), 0), 0)
    per_e = lambda i, be, nu, bv: (be[i], 0, 0)
    return pl.pallas_call(
        _expert_kernel,
        grid_spec=pltpu.PrefetchScalarGridSpec(
            num_scalar_prefetch=3,
            grid=(n_rows // tm,),
            in_specs=[pl.BlockSpec((tm, dp), rows),
                      pl.BlockSpec((1, d, dh), per_e),
                      pl.BlockSpec((1, 1, dh), per_e),
                      pl.BlockSpec((1, de, d), per_e),
                      pl.BlockSpec((1, 1, d), per_e)],
            out_specs=pl.BlockSpec((tm, d), lambda i, be, nu, bv: (i, 0)),
            scratch_shapes=[pltpu.VMEM((d, dh), BF16), pltpu.VMEM((de, d), BF16)]),
        out_shape=jax.ShapeDtypeStruct((n_rows, d), F32),
        compiler_params=pltpu.CompilerParams(
            dimension_semantics=("arbitrary",), vmem_limit_bytes=VMEM_LIMIT),
        name="moe_experts",
    )(block_e, n_used, block_valid, xs, w1, b1.reshape(ne, 1, dh), w2, b2.reshape(ne, 1, d))


def _combine_kernel(dest_ref, x2_ref, route_ref, g_ref, ys_ref, o_ref, buf_ref, sem):
    tm = x2_ref.shape[0]

    def issue(t, c):
        for k in range(TOP_K):
            _row_copy(ys_ref, dest_ref[t * TOP_K + k], buf_ref.at[k], t, sem).start(priority=k % 2)
        return c

    def drain(t, c):
        for k in range(TOP_K):
            _row_copy(ys_ref, dest_ref[t * TOP_K + k], buf_ref.at[k], t, sem).wait()
        return c

    lax.fori_loop(0, tm, issue, 0, unroll=8)
    lax.fori_loop(0, tm, drain, 0)
    route = route_ref[...]
    x3 = x2_ref[...]
    for k in range(TOP_K):
        x3 = x3 + buf_ref[k] * route[:, TOP_K + k:TOP_K + k + 1]
    o_ref[...] = _rms(x3, g_ref[...])


def _combine(dest_flat, x2, route, g_final, ys):
    n, d = x2.shape
    tm = ROW_TILE
    return pl.pallas_call(
        _combine_kernel,
        grid=(n // tm,),
        in_specs=[pl.BlockSpec((tm * TOP_K,), lambda i: (i,), memory_space=pltpu.SMEM),
                  pl.BlockSpec((tm, d), lambda i: (i, 0)),
                  pl.BlockSpec((tm, LANES), lambda i: (i, 0)),
                  pl.BlockSpec((1, d), lambda i: (0, 0)),
                  pl.BlockSpec(memory_space=pl.ANY)],
        out_specs=pl.BlockSpec((tm, d), lambda i: (i, 0)),
        out_shape=jax.ShapeDtypeStruct((n, d), F32),
        scratch_shapes=[pltpu.VMEM((TOP_K, tm, d), F32), pltpu.SemaphoreType.DMA],
        compiler_params=pltpu.CompilerParams(
            dimension_semantics=("arbitrary",), vmem_limit_bytes=VMEM_LIMIT),
        name="moe_combine_norm",
    )(dest_flat, x2, route, g_final, ys)


def _layer(x, mem, g_mix, w_in, conv_w, g_conv_out, g_sb_out, w_out, g_xattn, g_mem,
           w_q_mem, w_kv_mem, w_o_mem, g_moe, w_router, b_router, w1, b1, w2, b2):
    b, s, d = x.shape
    n = b * s
    row2 = lambda g: g.reshape(1, -1)

    grp = jnp.arange(CONV_DIM) // CONV_GROUP_DIM
    gmat = jnp.where(grp[:, None] == grp[None, :], 1.0 / CONV_GROUP_DIM, 0.0).astype(BF16)
    ar = jnp.arange(SB_TILE)
    tri = (ar[:, None] > ar[None, :]).astype(BF16)
    tri = jnp.concatenate([tri, tri], axis=0)
    ar = jnp.arange(ROW_TILE)
    ltri = (ar[:, None] > ar[None, :]).astype(BF16)

    kv = _mem_kv(mem, row2(g_mem), w_kv_mem.astype(BF16))
    yc, q, k, v = _inproj(x, row2(g_mix), w_in.astype(BF16), conv_w, row2(g_conv_out), gmat)
    ys = _stick_breaking(q, k, v, tri, row2(g_sb_out))
    w_out_b = w_out.astype(BF16)
    wr_hi = w_router.astype(BF16)
    wr_lo = (w_router - wr_hi.astype(F32)).astype(BF16)
    x2, h3, route, counts = _mid(x, yc, ys, w_out_b[:CONV_DIM], w_out_b[CONV_DIM:], row2(g_xattn),
                                 w_q_mem.astype(BF16), kv, w_o_mem.astype(BF16), row2(g_moe),
                                 jnp.concatenate([wr_hi, wr_hi, wr_lo], axis=0), row2(b_router), ltri)

    te = EXPERT_TILE
    route = route.reshape(n, LANES)
    counts = counts.reshape(N_EXPERTS).astype(jnp.int32)
    padded = (counts + te - 1) // te * te
    padded_end = jnp.cumsum(padded)
    start = padded_end - padded
    top_i = route[:, 0:TOP_K].astype(jnp.int32)
    rank = route[:, 2 * TOP_K:3 * TOP_K].astype(jnp.int32)
    experts = jnp.arange(N_EXPERTS, dtype=jnp.int32)
    first = jnp.sum(jnp.where(top_i[:, :, None] == experts, start, 0), axis=-1)
    dest = (first + rank).reshape(n * TOP_K)
    n_rows = n * TOP_K + N_EXPERTS * te
    n_blocks = n_rows // te
    n_used = (padded_end[-1] // te).astype(jnp.int32)
    blk = jnp.minimum(jnp.arange(n_blocks, dtype=jnp.int32), n_used - 1) * te
    block_e = jnp.minimum(jnp.sum((padded_end[None, :] <= blk[:, None]).astype(jnp.int32), axis=1),
                          N_EXPERTS - 1)

    xs = _dispatch(counts, padded, start, dest, h3.reshape(n, d), n_rows)
    block_valid = jnp.sum(jnp.where(block_e[:, None] == experts, counts - (blk[:, None] - start), 0), axis=1)
    yo = _experts(block_e, n_used.reshape(1), block_valid.astype(jnp.int32), xs, w1, b1, w2, b2)
    return dest, x2.reshape(n, d), route, yo


def kernel(x, mem, g_mix, w_in, conv_w, g_conv_out, g_sb_out, w_out, g_xattn, g_mem, w_q_mem,
           w_kv_mem, w_o_mem, g_moe, w_router, b_router, w1, b1, w2, b2, g_final):
    b, s, d = x.shape
    assert w_in.shape[0] == 1, "single-layer stack"
    dest, x2, route, yo = _layer(x, mem, g_mix[0], w_in[0], conv_w[0], g_conv_out[0], g_sb_out[0],
                                 w_out[0], g_xattn[0], g_mem[0], w_q_mem[0], w_kv_mem[0],
                                 w_o_mem[0], g_moe[0], w_router[0], b_router[0],
                                 w1[0], b1[0], w2[0], b2[0])
    out = _combine(dest, x2, route, g_final.reshape(1, d), yo)
    return out.reshape(b, s, d)
```

```python
import jax
import jax.numpy as jnp
from jax import lax
from jax.experimental import pallas as pl
from jax.experimental.pallas import tpu as pltpu

F32 = jnp.float32
BF16 = jnp.bfloat16

EPS = 1e-5
LOG2E = 1.4426950408889634
CONV_DIM = 512
SB_DIM = 512
SB_HEAD_DIM = 64
CONV_GROUP_DIM = 64
X_HEADS = 4
N_EXPERTS = 32
TOP_K = 4
SWIGLU_LIMIT = 7.0
SWIGLU_ALPHA = 1.702

LANES = 128
ROW_TILE = 512
SB_TILE = 256
EXPERT_TILE = 512
EXPERT_CHUNK = 256
VMEM_LIMIT = 56 * 1024 * 1024


def _rms(x, g):
    return x * lax.rsqrt(jnp.mean(x * x, axis=-1, keepdims=True) + EPS) * g


def _split_dot(a, m):
    hi = a.astype(BF16)
    lo = (a - hi.astype(F32)).astype(BF16)
    return (jnp.dot(hi, m, preferred_element_type=F32)
            + jnp.dot(lo, m, preferred_element_type=F32))


def _mem_kv_kernel(mem_ref, g_ref, w_ref, kv_ref):
    h = _rms(mem_ref[0], g_ref[...]).astype(BF16)
    kv_ref[0] = jnp.dot(h, w_ref[...], preferred_element_type=F32).astype(BF16)


def _mem_kv(mem, g_mem, w_kv):
    b, m, d = mem.shape
    return pl.pallas_call(
        _mem_kv_kernel,
        grid=(b,),
        in_specs=[pl.BlockSpec((1, m, d), lambda i: (i, 0, 0)),
                  pl.BlockSpec((1, d), lambda i: (0, 0)),
                  pl.BlockSpec((d, 2 * d), lambda i: (0, 0))],
        out_specs=pl.BlockSpec((1, m, 2 * d), lambda i: (i, 0, 0)),
        out_shape=jax.ShapeDtypeStruct((b, m, 2 * d), BF16),
        compiler_params=pltpu.CompilerParams(
            dimension_semantics=("arbitrary",), vmem_limit_bytes=VMEM_LIMIT),
        name="mem_kv",
    )(mem, g_mem, w_kv)


def _inproj_kernel(x_ref, g_ref, w_ref, cw_ref, gc_ref, gmat_ref,
                   yc_ref, q_ref, k_ref, v_ref, prev_ref):
    tm = x_ref.shape[1]
    h = _rms(x_ref[0], g_ref[...]).astype(BF16)
    proj = jnp.dot(h, w_ref[...], preferred_element_type=F32)
    c = CONV_DIM
    u = proj[:, c:2 * c] * proj[:, 2 * c:3 * c]

    @pl.when(pl.program_id(1) == 0)
    def _():
        prev_ref[...] = jnp.zeros_like(prev_ref)

    rows = lax.broadcasted_iota(jnp.int32, (tm, 1), 0)
    p1 = prev_ref[7:8, :]
    p2 = prev_ref[6:7, :]
    u1 = jnp.where(rows == 0, p1, pltpu.roll(u, 1, axis=0))
    u2 = jnp.where(rows == 0, p2, jnp.where(rows == 1, p1, pltpu.roll(u, 2, axis=0)))
    prev_ref[...] = u[tm - 8:tm, :]
    y = proj[:, 0:c] * (cw_ref[0:1, :] * u2 + cw_ref[1:2, :] * u1 + cw_ref[2:3, :] * u)
    ms = _split_dot(y * y, gmat_ref[...])
    yc_ref[0] = (y * lax.rsqrt(ms + EPS) * gc_ref[...]).astype(BF16)
    o = 3 * c
    q_ref[0] = (proj[:, o:o + SB_DIM] * (SB_HEAD_DIM ** -0.5)).astype(BF16)
    k_ref[0] = proj[:, o + SB_DIM:o + 2 * SB_DIM].astype(BF16)
    v_ref[0] = proj[:, o + 2 * SB_DIM:o + 3 * SB_DIM].astype(BF16)


def _inproj(x, g_mix, w_in, conv_w, g_conv, gmat):
    b, s, d = x.shape
    tm = ROW_TILE
    p = w_in.shape[1]
    row = lambda i, j: (i, j, 0)
    fixed = lambda i, j: (0, 0)
    outs = [jax.ShapeDtypeStruct((b, s, CONV_DIM), BF16)] * 4
    return pl.pallas_call(
        _inproj_kernel,
        grid=(b, s // tm),
        in_specs=[pl.BlockSpec((1, tm, d), row),
                  pl.BlockSpec((1, d), fixed),
                  pl.BlockSpec((d, p), fixed),
                  pl.BlockSpec(conv_w.shape, fixed),
                  pl.BlockSpec((1, CONV_DIM), fixed),
                  pl.BlockSpec((CONV_DIM, CONV_DIM), fixed)],
        out_specs=[pl.BlockSpec((1, tm, CONV_DIM), row)] * 4,
        out_shape=outs,
        scratch_shapes=[pltpu.VMEM((8, CONV_DIM), F32)],
        compiler_params=pltpu.CompilerParams(
            dimension_semantics=("arbitrary", "arbitrary"), vmem_limit_bytes=VMEM_LIMIT),
        name="inproj_conv",
    )(x, g_mix, w_in, conv_w, g_conv, gmat)


def _sb_kernel(q_ref, k_ref, v_ref, tri_ref, g_ref, o_ref):
    tq = tri_ref.shape[1]
    tk = tq
    lane = lax.broadcasted_iota(jnp.int32, (1, LANES), 1)
    in_a = lane < SB_HEAD_DIM
    tri = tri_ref[...]

    def scores(qh, kb):
        return lax.dot_general(qh, kb, (((1,), (1,)), ((), ())), preferred_element_type=F32) * LOG2E

    def later_keys(z2, valid):
        sp2 = jnp.maximum(z2, 0.0) + jnp.log2(1.0 + jnp.exp2(-jnp.abs(z2)))
        if valid is not None:
            sp2 = jnp.where(valid, sp2, 0.0)
        hi = sp2.astype(BF16)
        lo = (sp2 - hi.astype(F32)).astype(BF16)
        rest = jnp.dot(jnp.concatenate([hi, lo], axis=1), tri,
                       preferred_element_type=F32)
        return sp2, rest

    def weights(z2, sp2, rest, run, valid):
        a = jnp.exp2(z2 - sp2 - rest - run)
        if valid is not None:
            a = jnp.where(valid, a, 0.0)
        return a.astype(BF16), run + rest[:, 0:1] + sp2[:, 0:1]

    def tiles(q_heads, js, carry, valid):
        run = [carry[0], carry[2]]
        acc = [carry[1], carry[3]]
        kbs, vbs = [], []
        for j in js:
            start = pl.multiple_of(j * tk, tk)
            kbs.append(k_ref[0, pl.ds(start, tk), :])
            vbs.append(v_ref[0, pl.ds(start, tk), :])
        zs = [[scores(qh, kb) for qh in q_heads] for kb in kbs]
        mids = [[later_keys(z, valid) for z in zt] for zt in zs]
        for t in range(len(js)):
            for h in range(2):
                w, run[h] = weights(zs[t][h], *mids[t][h], run[h], valid)
                acc[h] = acc[h] + jnp.dot(w, vbs[t], preferred_element_type=F32)
        return run[0], acc[0], run[1], acc[1]

    r0 = jnp.zeros((tq, 1), F32)
    a0 = jnp.zeros((tq, LANES), F32)
    t_idx = lax.broadcasted_iota(jnp.int32, (tq, tk), 0)
    s_idx = lax.broadcasted_iota(jnp.int32, (tq, tk), 1)

    def query_tile(qi, _):
        rows = pl.ds(pl.multiple_of(qi * tq, tq), tq)
        q = q_ref[0, rows, :]
        zero = jnp.zeros_like(q)
        q_heads = (jnp.where(in_a, q, zero), jnp.where(in_a, zero, q))
        carry = tiles(q_heads, [qi], (r0, a0, r0, a0), s_idx < t_idx)
        odd = qi % 2
        carry = lax.cond(odd == 1, lambda c: tiles(q_heads, [qi - 1], c, None), lambda c: c, carry)
        first = qi - 1 - odd
        carry = lax.fori_loop(
            0, qi // 2,
            lambda n, c: tiles(q_heads, [first - 2 * n, first - 2 * n - 1], c, None), carry)
        out = jnp.where(in_a, carry[1], carry[3])
        sq = out * out
        ss_a = jnp.sum(jnp.where(in_a, sq, 0.0), axis=-1, keepdims=True)
        ss_b = jnp.sum(jnp.where(in_a, 0.0, sq), axis=-1, keepdims=True)
        ms = jnp.where(in_a, ss_a, ss_b) * (1.0 / SB_HEAD_DIM)
        o_ref[0, rows, :] = (out * lax.rsqrt(ms + EPS) * g_ref[...]).astype(BF16)
        return 0

    lax.fori_loop(0, q_ref.shape[1] // tq, query_tile, 0)


def _stick_breaking(q, k, v, tri, g_sb):
    b, s, w = q.shape
    tq = SB_TILE
    return pl.pallas_call(
        _sb_kernel,
        grid=(b, w // LANES),
        in_specs=[pl.BlockSpec((1, s, LANES), lambda i, h: (i, 0, h)),
                  pl.BlockSpec((1, s, LANES), lambda i, h: (i, 0, h)),
                  pl.BlockSpec((1, s, LANES), lambda i, h: (i, 0, h)),
                  pl.BlockSpec((2 * tq, tq), lambda i, h: (0, 0)),
                  pl.BlockSpec((1, LANES), lambda i, h: (0, h))],
        out_specs=pl.BlockSpec((1, s, LANES), lambda i, h: (i, 0, h)),
        out_shape=jax.ShapeDtypeStruct((b, s, w), BF16),
        compiler_params=pltpu.CompilerParams(
            dimension_semantics=("arbitrary", "arbitrary"),
            vmem_limit_bytes=VMEM_LIMIT),
        name="stick_breaking",
    )(q, k, v, tri, g_sb)


def _mid_kernel(x_ref, yc_ref, ys_ref, wo1_ref, wo2_ref, gx_ref, wq_ref, kv_ref, wom_ref,
                gm_ref, wr_ref, br_ref, ltri_ref,
                x2_ref, h3_ref, route_ref, cnt_ref, carry_ref):
    tm = x_ref.shape[1]
    d = x_ref.shape[2]
    hd = d // X_HEADS

    @pl.when((pl.program_id(0) == 0) & (pl.program_id(1) == 0))
    def _():
        carry_ref[...] = jnp.zeros_like(carry_ref)

    x1 = (x_ref[0]
          + jnp.dot(yc_ref[0], wo1_ref[...], preferred_element_type=F32)
          + jnp.dot(ys_ref[0], wo2_ref[...], preferred_element_type=F32))
    h2 = _rms(x1, gx_ref[...]).astype(BF16)
    qm = (jnp.dot(h2, wq_ref[...], preferred_element_type=F32) * (hd ** -0.5)).astype(BF16)
    heads = []
    for h in range(X_HEADS):
        kh = kv_ref[0, :, h * hd:(h + 1) * hd]
        vh = kv_ref[0, :, d + h * hd:d + (h + 1) * hd]
        sc = lax.dot_general(qm[:, h * hd:(h + 1) * hd], kh, (((1,), (1,)), ((), ())),
                             preferred_element_type=F32)
        e = jnp.exp(sc - jnp.max(sc, axis=-1, keepdims=True))
        p = e / jnp.sum(e, axis=-1, keepdims=True)
        heads.append(jnp.dot(p.astype(BF16), vh, preferred_element_type=F32).astype(BF16))
    x2 = x1 + jnp.dot(jnp.concatenate(heads, axis=-1), wom_ref[...], preferred_element_type=F32)
    x2_ref[0] = x2
    h3 = _rms(x2, gm_ref[...])
    h3_ref[0] = h3

    h_hi = h3.astype(BF16)
    h_lo = (h3 - h_hi.astype(F32)).astype(BF16)
    logits = jnp.dot(jnp.concatenate([h_hi, h_lo, h_hi], axis=1), wr_ref[...],
                     preferred_element_type=F32) + br_ref[...]
    lane = lax.broadcasted_iota(jnp.int32, (tm, N_EXPERTS), 1).astype(F32)
    vals, idxs = [], []
    l = logits
    for _ in range(TOP_K):
        m = jnp.max(l, axis=-1, keepdims=True)
        idx = jnp.min(jnp.where(l == m, lane, float(N_EXPERTS)), axis=-1, keepdims=True)
        vals.append(m)
        idxs.append(idx)
        l = jnp.where(lane == idx, -jnp.inf, l)
    es = [jnp.exp(vk - vals[0]) for vk in vals]
    den = es[0] + es[1] + es[2] + es[3]
    gates = [ek / den for ek in es]

    sel = jnp.zeros((tm, N_EXPERTS), F32)
    for idx in idxs:
        sel = sel + (lane == idx).astype(F32)
    before = jnp.dot(ltri_ref[...], sel.astype(BF16), preferred_element_type=F32) + carry_ref[...]
    ranks = [jnp.sum(jnp.where(lane == idx, before, 0.0), axis=-1, keepdims=True) for idx in idxs]
    carry_ref[...] += jnp.sum(sel, axis=0, keepdims=True)
    cnt_ref[...] = carry_ref[...]

    lane_o = lax.broadcasted_iota(jnp.int32, (tm, LANES), 1)
    r = jnp.zeros((tm, LANES), F32)
    for k in range(TOP_K):
        r = jnp.where(lane_o == k, idxs[k], r)
        r = jnp.where(lane_o == TOP_K + k, gates[k], r)
        r = jnp.where(lane_o == 2 * TOP_K + k, ranks[k], r)
    route_ref[0] = r


def _mid(x, yc, ys, wo1, wo2, g_x, w_q, kv, w_om, g_moe, w_r, b_r, ltri):
    b, s, d = x.shape
    tm = ROW_TILE
    m = kv.shape[1]
    row = lambda i, j: (i, j, 0)
    fixed = lambda i, j: (0, 0)
    return pl.pallas_call(
        _mid_kernel,
        grid=(b, s // tm),
        in_specs=[pl.BlockSpec((1, tm, d), row),
                  pl.BlockSpec((1, tm, CONV_DIM), row),
                  pl.BlockSpec((1, tm, SB_DIM), row),
                  pl.BlockSpec((CONV_DIM, d), fixed),
                  pl.BlockSpec((SB_DIM, d), lambda i, j: (CONV_DIM // SB_DIM, 0)),
                  pl.BlockSpec((1, d), fixed),
                  pl.BlockSpec((d, d), fixed),
                  pl.BlockSpec((1, m, 2 * d), lambda i, j: (i, 0, 0)),
                  pl.BlockSpec((d, d), fixed),
                  pl.BlockSpec((1, d), fixed),
                  pl.BlockSpec((3 * d, N_EXPERTS), fixed),
                  pl.BlockSpec((1, N_EXPERTS), fixed),
                  pl.BlockSpec((tm, tm), fixed)],
        out_specs=[pl.BlockSpec((1, tm, d), row),
                   pl.BlockSpec((1, tm, d), row),
                   pl.BlockSpec((1, tm, LANES), row),
                   pl.BlockSpec((1, N_EXPERTS), fixed)],
        out_shape=[jax.ShapeDtypeStruct((b, s, d), F32),
                   jax.ShapeDtypeStruct((b, s, d), F32),
                   jax.ShapeDtypeStruct((b, s, LANES), F32),
                   jax.ShapeDtypeStruct((1, N_EXPERTS), F32)],
        scratch_shapes=[pltpu.VMEM((1, N_EXPERTS), F32)],
        compiler_params=pltpu.CompilerParams(
            dimension_semantics=("arbitrary", "arbitrary"), vmem_limit_bytes=VMEM_LIMIT),
        name="outproj_xattn_router",
    )(x, yc, ys, wo1, wo2, g_x, w_q, kv, w_om, g_moe, w_r, b_r, ltri)


def _row_copy(src, src_row, dst, dst_row, sem):
    return pltpu.make_async_copy(src.at[pl.ds(src_row, 1), :], dst.at[pl.ds(dst_row, 1), :], sem)


def _dispatch_kernel(cnt_ref, pad_ref, start_ref, dest_ref, h_ref, zero_ref, xs_ref, stage_ref, sem, zsem):
    i = pl.program_id(0)
    n_steps = pl.num_programs(0)
    tm = h_ref.shape[0]
    te = zero_ref.shape[0]
    slot = i % 2

    def block_copy(row):
        rows = pl.ds(pl.multiple_of(row, te), te)
        return pltpu.make_async_copy(zero_ref, xs_ref.at[rows, :], zsem)

    @pl.when(i == 0)
    def _():
        last = N_EXPERTS - 1
        n_used = (start_ref[last] + pad_ref[last]) // te
        n_blocks = xs_ref.shape[0] // te

        def fill(wait):
            def expert(e, c):
                @pl.when(pad_ref[e] > cnt_ref[e])
                def _():
                    cp = block_copy(start_ref[e] + pad_ref[e] - te)
                    cp.wait() if wait else cp.start()
                return c

            def tail(j, c):
                cp = block_copy(j * te)
                cp.wait() if wait else cp.start()
                return c

            lax.fori_loop(0, N_EXPERTS, expert, 0)
            lax.fori_loop(n_used, n_blocks, tail, 0)

        fill(False)
        fill(True)

    def drain(s):
        def body(t, c):
            for k in range(TOP_K):
                _row_copy(stage_ref.at[s], t, xs_ref, 0, sem.at[s]).wait()
            return c
        lax.fori_loop(0, tm, body, 0)

    @pl.when(i >= 2)
    def _():
        drain(slot)

    stage_ref[slot] = h_ref[...]

    def issue(t, c):
        for k in range(TOP_K):
            _row_copy(stage_ref.at[slot], t, xs_ref, dest_ref[t * TOP_K + k],
                      sem.at[slot]).start(priority=k % 2)
        return c

    lax.fori_loop(0, tm, issue, 0, unroll=8)

    @pl.when(i == n_steps - 1)
    def _():
        @pl.when(n_steps > 1)
        def _():
            drain(1 - slot)
        drain(slot)


def _dispatch(counts, padded, start, dest_flat, h3, n_rows):
    n, d = h3.shape
    tm = ROW_TILE
    zero = jnp.zeros((EXPERT_TILE, d), h3.dtype)
    return pl.pallas_call(
        _dispatch_kernel,
        grid_spec=pltpu.PrefetchScalarGridSpec(
            num_scalar_prefetch=3,
            grid=(n // tm,),
            in_specs=[pl.BlockSpec((tm * TOP_K,), lambda i, *_: (i,), memory_space=pltpu.SMEM),
                      pl.BlockSpec((tm, d), lambda i, *_: (i, 0)),
                      pl.BlockSpec(zero.shape, lambda i, *_: (0, 0))],
            out_specs=pl.BlockSpec(memory_space=pl.ANY),
            scratch_shapes=[pltpu.VMEM((2, tm, d), h3.dtype), pltpu.SemaphoreType.DMA((2,)),
                            pltpu.SemaphoreType.DMA]),
        out_shape=jax.ShapeDtypeStruct((n_rows, d), h3.dtype),
        compiler_params=pltpu.CompilerParams(
            dimension_semantics=("arbitrary",), vmem_limit_bytes=VMEM_LIMIT),
        name="moe_dispatch",
    )(counts, padded, start, dest_flat, h3, zero)


def _expert_kernel(be_ref, nu_ref, bv_ref, slot_ref, nxt_ref, xs_ref, w1_ref, b1_ref, w2_ref, b2_ref, o_ref,
                   w1c_ref, w2c_ref, w1f_ref, w2f_ref, wsem):
    i = pl.program_id(0)
    tm = xs_ref.shape[0]
    de = w2_ref.shape[1]

    def weight_copies(e, s):
        return (pltpu.make_async_copy(w1_ref.at[e], w1f_ref.at[s], wsem.at[s, 0]),
                pltpu.make_async_copy(w2_ref.at[e], w2f_ref.at[s], wsem.at[s, 1]))

    @pl.when(i >= nu_ref[0])
    def _():
        o_ref[...] = jnp.zeros_like(o_ref)

    @pl.when(i < nu_ref[0])
    def _():
        changed = (i == 0) | (be_ref[i] != be_ref[jnp.maximum(i - 1, 0)])

        @pl.when(changed)
        def _():
            e = be_ref[i]
            s = slot_ref[i]

            @pl.when(i == 0)
            def _():
                for cp in weight_copies(e, s):
                    cp.start()

            for cp in weight_copies(e, s):
                cp.wait()
            w1c_ref[...] = w1f_ref[s].astype(BF16)
            w2c_ref[...] = w2f_ref[s].astype(BF16)

            @pl.when(nxt_ref[i] >= 0)
            def _():
                for cp in weight_copies(nxt_ref[i], 1 - s):
                    cp.start()

        ch = EXPERT_CHUNK

        def activation(xb, c):
            g = (jnp.dot(xb, w1c_ref[:, c * ch:(c + 1) * ch], preferred_element_type=F32)
                 + b1_ref[0, :, c * ch:(c + 1) * ch])
            l = (jnp.dot(xb, w1c_ref[:, de + c * ch:de + (c + 1) * ch], preferred_element_type=F32)
                 + b1_ref[0, :, de + c * ch:de + (c + 1) * ch])
            glu = jnp.minimum(g, SWIGLU_LIMIT)
            lin = jnp.clip(l, -SWIGLU_LIMIT, SWIGLU_LIMIT)
            half = 0.5 * glu
            return ((half + half * jnp.tanh((0.5 * SWIGLU_ALPHA) * glu)) * (lin + 1.0)).astype(BF16)

        def mlp(rows):
            xb = xs_ref[0:rows, :].astype(BF16)
            out = b2_ref[0]
            act = activation(xb, 0)
            for c in range(de // ch):
                nxt = activation(xb, c + 1) if (c + 1) * ch < de else None
                out = out + jnp.dot(act, w2c_ref[c * ch:(c + 1) * ch, :], preferred_element_type=F32)
                act = nxt
            o_ref[0:rows, :] = out
            if rows < tm:
                o_ref[rows:tm, :] = jnp.zeros((tm - rows, o_ref.shape[1]), F32)

        @pl.when(bv_ref[i] > tm // 2)
        def _():
            mlp(tm)

        @pl.when(bv_ref[i] <= tm // 2)
        def _():
            mlp(tm // 2)


def _experts(block_e, n_used, block_valid, block_slot, block_next, xs, w1, b1, w2, b2):
    n_rows, dp = xs.shape
    tm = EXPERT_TILE
    ne, d, dh = w1.shape
    de = w2.shape[1]
    rows = lambda i, be, nu, *_: (jnp.maximum(jnp.minimum(i, nu[0] - 1), 0), 0)
    per_e = lambda i, be, *_: (be[i], 0, 0)
    return pl.pallas_call(
        _expert_kernel,
        grid_spec=pltpu.PrefetchScalarGridSpec(
            num_scalar_prefetch=5,
            grid=(n_rows // tm,),
            in_specs=[pl.BlockSpec((tm, dp), rows),
                      pl.BlockSpec(memory_space=pl.ANY),
                      pl.BlockSpec((1, 1, dh), per_e),
                      pl.BlockSpec(memory_space=pl.ANY),
                      pl.BlockSpec((1, 1, d), per_e)],
            out_specs=pl.BlockSpec((tm, d), lambda i, *_: (i, 0)),
            scratch_shapes=[pltpu.VMEM((d, dh), BF16), pltpu.VMEM((de, d), BF16),
                            pltpu.VMEM((2, d, dh), F32), pltpu.VMEM((2, de, d), F32),
                            pltpu.SemaphoreType.DMA((2, 2))]),
        out_shape=jax.ShapeDtypeStruct((n_rows, d), F32),
        compiler_params=pltpu.CompilerParams(
            dimension_semantics=("arbitrary",), vmem_limit_bytes=VMEM_LIMIT),
        name="moe_experts",
    )(block_e, n_used, block_valid, block_slot, block_next, xs, w1, b1.reshape(ne, 1, dh), w2,
      b2.reshape(ne, 1, d))


def _combine_kernel(dest_ref, x2_ref, route_ref, g_ref, ys_ref, o_ref, buf_ref, sem):
    tm = x2_ref.shape[0]
    half = tm // 2

    def gather(r0, s, wait):
        def body(t, c):
            for k in range(TOP_K):
                cp = _row_copy(ys_ref, dest_ref[t * TOP_K + k], buf_ref.at[k], t, sem.at[s])
                cp.wait() if wait else cp.start(priority=k % 2)
            return c
        if wait:
            lax.fori_loop(r0, r0 + half, body, 0)
        else:
            lax.fori_loop(r0, r0 + half, body, 0, unroll=8)

    def finish(r0):
        route = route_ref[r0:r0 + half, :]
        x3 = x2_ref[r0:r0 + half, :]
        for k in range(TOP_K):
            x3 = x3 + buf_ref[k, r0:r0 + half, :] * route[:, TOP_K + k:TOP_K + k + 1]
        o_ref[r0:r0 + half, :] = _rms(x3, g_ref[...])

    gather(0, 0, False)
    gather(half, 1, False)
    gather(0, 0, True)
    finish(0)
    gather(half, 1, True)
    finish(half)


def _combine(dest_flat, x2, route, g_final, ys):
    n, d = x2.shape
    tm = ROW_TILE
    return pl.pallas_call(
        _combine_kernel,
        grid=(n // tm,),
        in_specs=[pl.BlockSpec((tm * TOP_K,), lambda i: (i,), memory_space=pltpu.SMEM),
                  pl.BlockSpec((tm, d), lambda i: (i, 0)),
                  pl.BlockSpec((tm, LANES), lambda i: (i, 0)),
                  pl.BlockSpec((1, d), lambda i: (0, 0)),
                  pl.BlockSpec(memory_space=pl.ANY)],
        out_specs=pl.BlockSpec((tm, d), lambda i: (i, 0)),
        out_shape=jax.ShapeDtypeStruct((n, d), F32),
        scratch_shapes=[pltpu.VMEM((TOP_K, tm, d), F32), pltpu.SemaphoreType.DMA((2,))],
        compiler_params=pltpu.CompilerParams(
            dimension_semantics=("arbitrary",), vmem_limit_bytes=VMEM_LIMIT),
        name="moe_combine_norm",
    )(dest_flat, x2, route, g_final, ys)


def _layer(x, mem, g_mix, w_in, conv_w, g_conv_out, g_sb_out, w_out, g_xattn, g_mem,
           w_q_mem, w_kv_mem, w_o_mem, g_moe, w_router, b_router, w1, b1, w2, b2):
    b, s, d = x.shape
    n = b * s
    row2 = lambda g: g.reshape(1, -1)

    grp = jnp.arange(CONV_DIM) // CONV_GROUP_DIM
    gmat = jnp.where(grp[:, None] == grp[None, :], 1.0 / CONV_GROUP_DIM, 0.0).astype(BF16)
    ar = jnp.arange(SB_TILE)
    tri = (ar[:, None] > ar[None, :]).astype(BF16)
    tri = jnp.concatenate([tri, tri], axis=0)
    ar = jnp.arange(ROW_TILE)
    ltri = (ar[:, None] > ar[None, :]).astype(BF16)

    kv = _mem_kv(mem, row2(g_mem), w_kv_mem.astype(BF16))
    yc, q, k, v = _inproj(x, row2(g_mix), w_in.astype(BF16), conv_w, row2(g_conv_out), gmat)
    ys = _stick_breaking(q, k, v, tri, row2(g_sb_out))
    w_out_b = w_out.astype(BF16)
    wr_hi = w_router.astype(BF16)
    wr_lo = (w_router - wr_hi.astype(F32)).astype(BF16)
    x2, h3, route, counts = _mid(x, yc, ys, w_out_b, w_out_b, row2(g_xattn),
                                 w_q_mem.astype(BF16), kv, w_o_mem.astype(BF16), row2(g_moe),
                                 jnp.concatenate([wr_hi, wr_hi, wr_lo], axis=0), row2(b_router), ltri)

    te = EXPERT_TILE
    route = route.reshape(n, LANES)
    counts = counts.reshape(N_EXPERTS).astype(jnp.int32)
    padded = (counts + te - 1) // te * te
    padded_end = jnp.cumsum(padded)
    start = padded_end - padded
    top_i = route[:, 0:TOP_K].astype(jnp.int32)
    rank = route[:, 2 * TOP_K:3 * TOP_K].astype(jnp.int32)
    experts = jnp.arange(N_EXPERTS, dtype=jnp.int32)
    first = jnp.sum(jnp.where(top_i[:, :, None] == experts, start, 0), axis=-1)
    dest = (first + rank).reshape(n * TOP_K)
    n_rows = n * TOP_K + N_EXPERTS * te
    n_blocks = n_rows // te
    n_used = (padded_end[-1] // te).astype(jnp.int32)
    blk = jnp.minimum(jnp.arange(n_blocks, dtype=jnp.int32), n_used - 1) * te
    block_e = jnp.minimum(jnp.sum((padded_end[None, :] <= blk[:, None]).astype(jnp.int32), axis=1),
                          N_EXPERTS - 1)

    xs = _dispatch(counts, padded, start, dest, h3.reshape(n, d), n_rows)
    block_valid = jnp.sum(jnp.where(block_e[:, None] == experts, counts - (blk[:, None] - start), 0), axis=1)
    nonempty = counts > 0
    slot_e = (jnp.cumsum(nonempty.astype(jnp.int32)) - 1) % 2
    later = (experts[None, :] > experts[:, None]) & nonempty[None, :]
    next_e = jnp.min(jnp.where(later, experts[None, :], N_EXPERTS), axis=1)
    next_e = jnp.where(next_e < N_EXPERTS, next_e, -1)
    of_block = block_e[:, None] == experts
    block_slot = jnp.sum(jnp.where(of_block, slot_e, 0), axis=1).astype(jnp.int32)
    block_next = jnp.sum(jnp.where(of_block, next_e, 0), axis=1).astype(jnp.int32)
    yo = _experts(block_e, n_used.reshape(1), block_valid.astype(jnp.int32), block_slot, block_next,
                  xs, w1, b1, w2, b2)
    return dest, x2.reshape(n, d), route, yo


def kernel(x, mem, g_mix, w_in, conv_w, g_conv_out, g_sb_out, w_out, g_xattn, g_mem, w_q_mem,
           w_kv_mem, w_o_mem, g_moe, w_router, b_router, w1, b1, w2, b2, g_final):
    b, s, d = x.shape
    assert w_in.shape[0] == 1, "single-layer stack"
    dest, x2, route, yo = _layer(x, mem, g_mix[0], w_in[0], conv_w[0], g_conv_out[0], g_sb_out[0],
                                 w_out[0], g_xattn[0], g_mem[0], w_q_mem[0], w_kv_mem[0],
                                 w_o_mem[0], g_moe[0], w_router[0], b_router[0],
                                 w1[0], b1[0], w2[0], b2[0])
    out = _combine(dest, x2, route, g_final.reshape(1, d), yo)
    return out.reshape(b, s, d)
```

```python
import jax
import jax.numpy as jnp
from jax import lax
from jax.experimental import pallas as pl
from jax.experimental.pallas import tpu as pltpu

F32 = jnp.float32
BF16 = jnp.bfloat16

EPS = 1e-5
LOG2E = 1.4426950408889634
CONV_DIM = 512
SB_DIM = 512
SB_HEAD_DIM = 64
CONV_GROUP_DIM = 64
X_HEADS = 4
N_EXPERTS = 32
TOP_K = 4
SWIGLU_LIMIT = 7.0
SWIGLU_ALPHA = 1.702

LANES = 128
ROW_TILE = 512
SB_TILE = 256
EXPERT_TILE = 512
EXPERT_CHUNK = 256
VMEM_LIMIT = 56 * 1024 * 1024


def _rms(x, g):
    return x * lax.rsqrt(jnp.mean(x * x, axis=-1, keepdims=True) + EPS) * g


def _split_dot(a, m):
    hi = a.astype(BF16)
    lo = (a - hi.astype(F32)).astype(BF16)
    return (jnp.dot(hi, m, preferred_element_type=F32)
            + jnp.dot(lo, m, preferred_element_type=F32))


def _mem_kv_kernel(mem_ref, g_ref, w_ref, kv_ref):
    h = _rms(mem_ref[0], g_ref[...]).astype(BF16)
    kv_ref[0] = jnp.dot(h, w_ref[...], preferred_element_type=F32).astype(BF16)


def _mem_kv(mem, g_mem, w_kv):
    b, m, d = mem.shape
    return pl.pallas_call(
        _mem_kv_kernel,
        grid=(b,),
        in_specs=[pl.BlockSpec((1, m, d), lambda i: (i, 0, 0)),
                  pl.BlockSpec((1, d), lambda i: (0, 0)),
                  pl.BlockSpec((d, 2 * d), lambda i: (0, 0))],
        out_specs=pl.BlockSpec((1, m, 2 * d), lambda i: (i, 0, 0)),
        out_shape=jax.ShapeDtypeStruct((b, m, 2 * d), BF16),
        compiler_params=pltpu.CompilerParams(
            dimension_semantics=("arbitrary",), vmem_limit_bytes=VMEM_LIMIT),
        name="mem_kv",
    )(mem, g_mem, w_kv)


def _inproj_kernel(x_ref, g_ref, w_ref, cw_ref, gc_ref, gmat_ref,
                   yc_ref, q_ref, k_ref, v_ref, prev_ref):
    tm = x_ref.shape[1]
    h = _rms(x_ref[0], g_ref[...]).astype(BF16)
    proj = jnp.dot(h, w_ref[...], preferred_element_type=F32)
    c = CONV_DIM
    u = proj[:, c:2 * c] * proj[:, 2 * c:3 * c]

    @pl.when(pl.program_id(1) == 0)
    def _():
        prev_ref[...] = jnp.zeros_like(prev_ref)

    rows = lax.broadcasted_iota(jnp.int32, (tm, 1), 0)
    p1 = prev_ref[7:8, :]
    p2 = prev_ref[6:7, :]
    u1 = jnp.where(rows == 0, p1, pltpu.roll(u, 1, axis=0))
    u2 = jnp.where(rows == 0, p2, jnp.where(rows == 1, p1, pltpu.roll(u, 2, axis=0)))
    prev_ref[...] = u[tm - 8:tm, :]
    y = proj[:, 0:c] * (cw_ref[0:1, :] * u2 + cw_ref[1:2, :] * u1 + cw_ref[2:3, :] * u)
    ms = _split_dot(y * y, gmat_ref[...])
    yc_ref[0] = (y * lax.rsqrt(ms + EPS) * gc_ref[...]).astype(BF16)
    o = 3 * c
    q_ref[0] = (proj[:, o:o + SB_DIM] * (SB_HEAD_DIM ** -0.5)).astype(BF16)
    k_ref[0] = proj[:, o + SB_DIM:o + 2 * SB_DIM].astype(BF16)
    v_ref[0] = proj[:, o + 2 * SB_DIM:o + 3 * SB_DIM].astype(BF16)


def _inproj(x, g_mix, w_in, conv_w, g_conv, gmat):
    b, s, d = x.shape
    tm = ROW_TILE
    p = w_in.shape[1]
    row = lambda i, j: (i, j, 0)
    fixed = lambda i, j: (0, 0)
    outs = [jax.ShapeDtypeStruct((b, s, CONV_DIM), BF16)] * 4
    return pl.pallas_call(
        _inproj_kernel,
        grid=(b, s // tm),
        in_specs=[pl.BlockSpec((1, tm, d), row),
                  pl.BlockSpec((1, d), fixed),
                  pl.BlockSpec((d, p), fixed),
                  pl.BlockSpec(conv_w.shape, fixed),
                  pl.BlockSpec((1, CONV_DIM), fixed),
                  pl.BlockSpec((CONV_DIM, CONV_DIM), fixed)],
        out_specs=[pl.BlockSpec((1, tm, CONV_DIM), row)] * 4,
        out_shape=outs,
        scratch_shapes=[pltpu.VMEM((8, CONV_DIM), F32)],
        compiler_params=pltpu.CompilerParams(
            dimension_semantics=("arbitrary", "arbitrary"), vmem_limit_bytes=VMEM_LIMIT),
        name="inproj_conv",
    )(x, g_mix, w_in, conv_w, g_conv, gmat)


def _sb_kernel(q_ref, k_ref, v_ref, tri_ref, g_ref, o_ref):
    tq = tri_ref.shape[1]
    tk = tq
    lane = lax.broadcasted_iota(jnp.int32, (1, LANES), 1)
    in_a = lane < SB_HEAD_DIM
    tri = tri_ref[...]

    def scores(qh, kb):
        return lax.dot_general(qh, kb, (((1,), (1,)), ((), ())), preferred_element_type=F32) * LOG2E

    def later_keys(z2, valid):
        sp2 = jnp.maximum(z2, 0.0) + jnp.log2(1.0 + jnp.exp2(-jnp.abs(z2)))
        if valid is not None:
            sp2 = jnp.where(valid, sp2, 0.0)
        hi = sp2.astype(BF16)
        lo = (sp2 - hi.astype(F32)).astype(BF16)
        rest = jnp.dot(jnp.concatenate([hi, lo], axis=1), tri,
                       preferred_element_type=F32)
        return sp2, rest

    def weights(z2, sp2, rest, run, valid):
        a = jnp.exp2(z2 - sp2 - rest - run)
        if valid is not None:
            a = jnp.where(valid, a, 0.0)
        return a.astype(BF16), run + rest[:, 0:1] + sp2[:, 0:1]

    def tiles(q_heads, js, carry, valids):
        run = [carry[0], carry[2]]
        acc = [carry[1], carry[3]]
        kbs, vbs = [], []
        for j in js:
            start = pl.multiple_of(j * tk, tk)
            kbs.append(k_ref[0, pl.ds(start, tk), :])
            vbs.append(v_ref[0, pl.ds(start, tk), :])
        zs = [[scores(qh, kb) for qh in q_heads] for kb in kbs]
        mids = [[later_keys(z, valid) for z in zt] for zt, valid in zip(zs, valids)]
        for t in range(len(js)):
            for h in range(2):
                w, run[h] = weights(zs[t][h], *mids[t][h], run[h], valids[t])
                acc[h] = acc[h] + jnp.dot(w, vbs[t], preferred_element_type=F32)
        return run[0], acc[0], run[1], acc[1]

    r0 = jnp.zeros((tq, 1), F32)
    a0 = jnp.zeros((tq, LANES), F32)
    t_idx = lax.broadcasted_iota(jnp.int32, (tq, tk), 0)
    s_idx = lax.broadcasted_iota(jnp.int32, (tq, tk), 1)

    def query_tile(qi, _):
        rows = pl.ds(pl.multiple_of(qi * tq, tq), tq)
        q = q_ref[0, rows, :]
        zero = jnp.zeros_like(q)
        q_heads = (jnp.where(in_a, q, zero), jnp.where(in_a, zero, q))
        diag = s_idx < t_idx
        start = (r0, a0, r0, a0)
        odd = qi % 2
        carry = lax.cond(odd == 1,
                         lambda c: tiles(q_heads, [qi, qi - 1], c, [diag, None]),
                         lambda c: tiles(q_heads, [qi], c, [diag]), start)
        first = qi - 1 - odd
        carry = lax.fori_loop(
            0, qi // 2,
            lambda n, c: tiles(q_heads, [first - 2 * n, first - 2 * n - 1], c, [None, None]), carry)
        out = jnp.where(in_a, carry[1], carry[3])
        sq = out * out
        ss_a = jnp.sum(jnp.where(in_a, sq, 0.0), axis=-1, keepdims=True)
        ss_b = jnp.sum(jnp.where(in_a, 0.0, sq), axis=-1, keepdims=True)
        ms = jnp.where(in_a, ss_a, ss_b) * (1.0 / SB_HEAD_DIM)
        o_ref[0, rows, :] = (out * lax.rsqrt(ms + EPS) * g_ref[...]).astype(BF16)
        return 0

    lax.fori_loop(0, q_ref.shape[1] // tq, query_tile, 0)


def _stick_breaking(q, k, v, tri, g_sb):
    b, s, w = q.shape
    tq = SB_TILE
    return pl.pallas_call(
        _sb_kernel,
        grid=(b, w // LANES),
        in_specs=[pl.BlockSpec((1, s, LANES), lambda i, h: (i, 0, h)),
                  pl.BlockSpec((1, s, LANES), lambda i, h: (i, 0, h)),
                  pl.BlockSpec((1, s, LANES), lambda i, h: (i, 0, h)),
                  pl.BlockSpec((2 * tq, tq), lambda i, h: (0, 0)),
                  pl.BlockSpec((1, LANES), lambda i, h: (0, h))],
        out_specs=pl.BlockSpec((1, s, LANES), lambda i, h: (i, 0, h)),
        out_shape=jax.ShapeDtypeStruct((b, s, w), BF16),
        compiler_params=pltpu.CompilerParams(
            dimension_semantics=("arbitrary", "arbitrary"),
            vmem_limit_bytes=VMEM_LIMIT),
        name="stick_breaking",
    )(q, k, v, tri, g_sb)


def _mid_kernel(x_ref, yc_ref, ys_ref, wo1_ref, wo2_ref, gx_ref, wq_ref, kv_ref, wom_ref,
                gm_ref, wr_ref, br_ref, ltri_ref,
                x2_ref, h3_ref, route_ref, cnt_ref, carry_ref):
    tm = x_ref.shape[1]
    d = x_ref.shape[2]
    hd = d // X_HEADS

    @pl.when((pl.program_id(0) == 0) & (pl.program_id(1) == 0))
    def _():
        carry_ref[...] = jnp.zeros_like(carry_ref)

    x1 = (x_ref[0]
          + jnp.dot(yc_ref[0], wo1_ref[...], preferred_element_type=F32)
          + jnp.dot(ys_ref[0], wo2_ref[...], preferred_element_type=F32))
    h2 = _rms(x1, gx_ref[...]).astype(BF16)
    qm = (jnp.dot(h2, wq_ref[...], preferred_element_type=F32) * (hd ** -0.5)).astype(BF16)
    heads = []
    for h in range(X_HEADS):
        kh = kv_ref[0, :, h * hd:(h + 1) * hd]
        vh = kv_ref[0, :, d + h * hd:d + (h + 1) * hd]
        sc = lax.dot_general(qm[:, h * hd:(h + 1) * hd], kh, (((1,), (1,)), ((), ())),
                             preferred_element_type=F32)
        e = jnp.exp(sc - jnp.max(sc, axis=-1, keepdims=True))
        p = e / jnp.sum(e, axis=-1, keepdims=True)
        heads.append(jnp.dot(p.astype(BF16), vh, preferred_element_type=F32).astype(BF16))
    x2 = x1 + jnp.dot(jnp.concatenate(heads, axis=-1), wom_ref[...], preferred_element_type=F32)
    x2_ref[0] = x2
    h3 = _rms(x2, gm_ref[...])
    h3_ref[0] = h3

    h_hi = h3.astype(BF16)
    h_lo = (h3 - h_hi.astype(F32)).astype(BF16)
    logits = jnp.dot(jnp.concatenate([h_hi, h_lo, h_hi], axis=1), wr_ref[...],
                     preferred_element_type=F32) + br_ref[...]
    lane = lax.broadcasted_iota(jnp.int32, (tm, N_EXPERTS), 1).astype(F32)
    vals, idxs = [], []
    l = logits
    for _ in range(TOP_K):
        m = jnp.max(l, axis=-1, keepdims=True)
        idx = jnp.min(jnp.where(l == m, lane, float(N_EXPERTS)), axis=-1, keepdims=True)
        vals.append(m)
        idxs.append(idx)
        l = jnp.where(lane == idx, -jnp.inf, l)
    es = [jnp.exp(vk - vals[0]) for vk in vals]
    den = es[0] + es[1] + es[2] + es[3]
    gates = [ek / den for ek in es]

    sel = jnp.zeros((tm, N_EXPERTS), F32)
    for idx in idxs:
        sel = sel + (lane == idx).astype(F32)
    before = jnp.dot(ltri_ref[...], sel.astype(BF16), preferred_element_type=F32) + carry_ref[...]
    ranks = [jnp.sum(jnp.where(lane == idx, before, 0.0), axis=-1, keepdims=True) for idx in idxs]
    carry_ref[...] += jnp.sum(sel, axis=0, keepdims=True)
    cnt_ref[...] = carry_ref[...]

    lane_o = lax.broadcasted_iota(jnp.int32, (tm, LANES), 1)
    r = jnp.zeros((tm, LANES), F32)
    for k in range(TOP_K):
        r = jnp.where(lane_o == k, idxs[k], r)
        r = jnp.where(lane_o == TOP_K + k, gates[k], r)
        r = jnp.where(lane_o == 2 * TOP_K + k, ranks[k], r)
    route_ref[0] = r


def _mid(x, yc, ys, wo1, wo2, g_x, w_q, kv, w_om, g_moe, w_r, b_r, ltri):
    b, s, d = x.shape
    tm = ROW_TILE
    m = kv.shape[1]
    row = lambda i, j: (i, j, 0)
    fixed = lambda i, j: (0, 0)
    return pl.pallas_call(
        _mid_kernel,
        grid=(b, s // tm),
        in_specs=[pl.BlockSpec((1, tm, d), row),
                  pl.BlockSpec((1, tm, CONV_DIM), row),
                  pl.BlockSpec((1, tm, SB_DIM), row),
                  pl.BlockSpec((CONV_DIM, d), fixed),
                  pl.BlockSpec((SB_DIM, d), lambda i, j: (CONV_DIM // SB_DIM, 0)),
                  pl.BlockSpec((1, d), fixed),
                  pl.BlockSpec((d, d), fixed),
                  pl.BlockSpec((1, m, 2 * d), lambda i, j: (i, 0, 0)),
                  pl.BlockSpec((d, d), fixed),
                  pl.BlockSpec((1, d), fixed),
                  pl.BlockSpec((3 * d, N_EXPERTS), fixed),
                  pl.BlockSpec((1, N_EXPERTS), fixed),
                  pl.BlockSpec((tm, tm), fixed)],
        out_specs=[pl.BlockSpec((1, tm, d), row),
                   pl.BlockSpec((1, tm, d), row),
                   pl.BlockSpec((1, tm, LANES), row),
                   pl.BlockSpec((1, N_EXPERTS), fixed)],
        out_shape=[jax.ShapeDtypeStruct((b, s, d), F32),
                   jax.ShapeDtypeStruct((b, s, d), F32),
                   jax.ShapeDtypeStruct((b, s, LANES), F32),
                   jax.ShapeDtypeStruct((1, N_EXPERTS), F32)],
        scratch_shapes=[pltpu.VMEM((1, N_EXPERTS), F32)],
        compiler_params=pltpu.CompilerParams(
            dimension_semantics=("arbitrary", "arbitrary"), vmem_limit_bytes=VMEM_LIMIT),
        name="outproj_xattn_router",
    )(x, yc, ys, wo1, wo2, g_x, w_q, kv, w_om, g_moe, w_r, b_r, ltri)


def _row_copy(src, src_row, dst, dst_row, sem):
    return pltpu.make_async_copy(src.at[pl.ds(src_row, 1), :], dst.at[pl.ds(dst_row, 1), :], sem)


def _dispatch_kernel(cnt_ref, pad_ref, start_ref, dest_ref, h_ref, zero_ref, xs_ref, stage_ref, sem, zsem):
    i = pl.program_id(0)
    n_steps = pl.num_programs(0)
    tm = h_ref.shape[0]
    te = zero_ref.shape[0]
    slot = i % 2

    def block_copy(row):
        rows = pl.ds(pl.multiple_of(row, te), te)
        return pltpu.make_async_copy(zero_ref, xs_ref.at[rows, :], zsem)

    @pl.when(i == 0)
    def _():
        last = N_EXPERTS - 1
        n_used = (start_ref[last] + pad_ref[last]) // te
        n_blocks = xs_ref.shape[0] // te

        def fill(wait):
            def expert(e, c):
                @pl.when(pad_ref[e] > cnt_ref[e])
                def _():
                    cp = block_copy(start_ref[e] + pad_ref[e] - te)
                    cp.wait() if wait else cp.start()
                return c

            def tail(j, c):
                cp = block_copy(j * te)
                cp.wait() if wait else cp.start()
                return c

            lax.fori_loop(0, N_EXPERTS, expert, 0)
            lax.fori_loop(n_used, n_blocks, tail, 0)

        fill(False)
        fill(True)

    def drain(s):
        def body(t, c):
            for k in range(TOP_K):
                _row_copy(stage_ref.at[s], t, xs_ref, 0, sem.at[s]).wait()
            return c
        lax.fori_loop(0, tm, body, 0)

    @pl.when(i >= 2)
    def _():
        drain(slot)

    stage_ref[slot] = h_ref[...]

    def issue(t, c):
        for k in range(TOP_K):
            _row_copy(stage_ref.at[slot], t, xs_ref, dest_ref[t * TOP_K + k],
                      sem.at[slot]).start(priority=k % 2)
        return c

    lax.fori_loop(0, tm, issue, 0, unroll=8)

    @pl.when(i == n_steps - 1)
    def _():
        @pl.when(n_steps > 1)
        def _():
            drain(1 - slot)
        drain(slot)


def _dispatch(counts, padded, start, dest_flat, h3, n_rows):
    n, d = h3.shape
    tm = ROW_TILE
    zero = jnp.zeros((EXPERT_TILE, d), h3.dtype)
    return pl.pallas_call(
        _dispatch_kernel,
        grid_spec=pltpu.PrefetchScalarGridSpec(
            num_scalar_prefetch=3,
            grid=(n // tm,),
            in_specs=[pl.BlockSpec((tm * TOP_K,), lambda i, *_: (i,), memory_space=pltpu.SMEM),
                      pl.BlockSpec((tm, d), lambda i, *_: (i, 0)),
                      pl.BlockSpec(zero.shape, lambda i, *_: (0, 0))],
            out_specs=pl.BlockSpec(memory_space=pl.ANY),
            scratch_shapes=[pltpu.VMEM((2, tm, d), h3.dtype), pltpu.SemaphoreType.DMA((2,)),
                            pltpu.SemaphoreType.DMA]),
        out_shape=jax.ShapeDtypeStruct((n_rows, d), h3.dtype),
        compiler_params=pltpu.CompilerParams(
            dimension_semantics=("arbitrary",), vmem_limit_bytes=VMEM_LIMIT),
        name="moe_dispatch",
    )(counts, padded, start, dest_flat, h3, zero)


def _expert_kernel(be_ref, nu_ref, bv_ref, slot_ref, nxt_ref, xs_ref, w1_ref, b1_ref, w2_ref, b2_ref, o_ref,
                   w1c_ref, w2c_ref, w1f_ref, w2f_ref, wsem):
    i = pl.program_id(0)
    tm = xs_ref.shape[0]
    de = w2_ref.shape[1]

    def weight_copies(e, s):
        return (pltpu.make_async_copy(w1_ref.at[e], w1f_ref.at[s], wsem.at[s, 0]),
                pltpu.make_async_copy(w2_ref.at[e], w2f_ref.at[s], wsem.at[s, 1]))

    @pl.when(i >= nu_ref[0])
    def _():
        o_ref[...] = jnp.zeros_like(o_ref)

    @pl.when(i < nu_ref[0])
    def _():
        changed = (i == 0) | (be_ref[i] != be_ref[jnp.maximum(i - 1, 0)])

        @pl.when(changed)
        def _():
            e = be_ref[i]
            s = slot_ref[i]

            @pl.when(i == 0)
            def _():
                for cp in weight_copies(e, s):
                    cp.start()

            for cp in weight_copies(e, s):
                cp.wait()
            w1c_ref[...] = w1f_ref[s].astype(BF16)
            w2c_ref[...] = w2f_ref[s].astype(BF16)

            @pl.when(nxt_ref[i] >= 0)
            def _():
                for cp in weight_copies(nxt_ref[i], 1 - s):
                    cp.start()

        ch = EXPERT_CHUNK

        def activation(xb, c):
            g = (jnp.dot(xb, w1c_ref[:, c * ch:(c + 1) * ch], preferred_element_type=F32)
                 + b1_ref[0, :, c * ch:(c + 1) * ch])
            l = (jnp.dot(xb, w1c_ref[:, de + c * ch:de + (c + 1) * ch], preferred_element_type=F32)
                 + b1_ref[0, :, de + c * ch:de + (c + 1) * ch])
            glu = jnp.minimum(g, SWIGLU_LIMIT)
            lin = jnp.clip(l, -SWIGLU_LIMIT, SWIGLU_LIMIT)
            half = 0.5 * glu
            return ((half + half * jnp.tanh((0.5 * SWIGLU_ALPHA) * glu)) * (lin + 1.0)).astype(BF16)

        def mlp(rows):
            xb = xs_ref[0:rows, :].astype(BF16)
            out = b2_ref[0]
            act = activation(xb, 0)
            for c in range(de // ch):
                nxt = activation(xb, c + 1) if (c + 1) * ch < de else None
                out = out + jnp.dot(act, w2c_ref[c * ch:(c + 1) * ch, :], preferred_element_type=F32)
                act = nxt
            o_ref[0:rows, :] = out
            if rows < tm:
                o_ref[rows:tm, :] = jnp.zeros((tm - rows, o_ref.shape[1]), F32)

        @pl.when(bv_ref[i] > tm // 2)
        def _():
            mlp(tm)

        @pl.when(bv_ref[i] <= tm // 2)
        def _():
            mlp(tm // 2)


def _experts(block_e, n_used, block_valid, block_slot, block_next, xs, w1, b1, w2, b2):
    n_rows, dp = xs.shape
    tm = EXPERT_TILE
    ne, d, dh = w1.shape
    de = w2.shape[1]
    rows = lambda i, be, nu, *_: (jnp.maximum(jnp.minimum(i, nu[0] - 1), 0), 0)
    per_e = lambda i, be, *_: (be[i], 0, 0)
    return pl.pallas_call(
        _expert_kernel,
        grid_spec=pltpu.PrefetchScalarGridSpec(
            num_scalar_prefetch=5,
            grid=(n_rows // tm,),
            in_specs=[pl.BlockSpec((tm, dp), rows),
                      pl.BlockSpec(memory_space=pl.ANY),
                      pl.BlockSpec((1, 1, dh), per_e),
                      pl.BlockSpec(memory_space=pl.ANY),
                      pl.BlockSpec((1, 1, d), per_e)],
            out_specs=pl.BlockSpec((tm, d), lambda i, *_: (i, 0)),
            scratch_shapes=[pltpu.VMEM((d, dh), BF16), pltpu.VMEM((de, d), BF16),
                            pltpu.VMEM((2, d, dh), F32), pltpu.VMEM((2, de, d), F32),
                            pltpu.SemaphoreType.DMA((2, 2))]),
        out_shape=jax.ShapeDtypeStruct((n_rows, d), F32),
        compiler_params=pltpu.CompilerParams(
            dimension_semantics=("arbitrary",), vmem_limit_bytes=VMEM_LIMIT),
        name="moe_experts",
    )(block_e, n_used, block_valid, block_slot, block_next, xs, w1, b1.reshape(ne, 1, dh), w2,
      b2.reshape(ne, 1, d))


def _combine_kernel(dest_ref, x2_ref, route_ref, g_ref, ys_ref, o_ref, buf_ref, sem):
    tm = x2_ref.shape[0]
    half = tm // 2

    def gather(r0, s, wait):
        def body(t, c):
            for k in range(TOP_K):
                cp = _row_copy(ys_ref, dest_ref[t * TOP_K + k], buf_ref.at[k], t, sem.at[s])
                cp.wait() if wait else cp.start(priority=k % 2)
            return c
        if wait:
            lax.fori_loop(r0, r0 + half, body, 0)
        else:
            lax.fori_loop(r0, r0 + half, body, 0, unroll=8)

    def finish(r0):
        route = route_ref[r0:r0 + half, :]
        x3 = x2_ref[r0:r0 + half, :]
        for k in range(TOP_K):
            x3 = x3 + buf_ref[k, r0:r0 + half, :] * route[:, TOP_K + k:TOP_K + k + 1]
        o_ref[r0:r0 + half, :] = _rms(x3, g_ref[...])

    gather(0, 0, False)
    gather(half, 1, False)
    gather(0, 0, True)
    finish(0)
    gather(half, 1, True)
    finish(half)


def _combine(dest_flat, x2, route, g_final, ys):
    n, d = x2.shape
    tm = ROW_TILE
    return pl.pallas_call(
        _combine_kernel,
        grid=(n // tm,),
        in_specs=[pl.BlockSpec((tm * TOP_K,), lambda i: (i,), memory_space=pltpu.SMEM),
                  pl.BlockSpec((tm, d), lambda i: (i, 0)),
                  pl.BlockSpec((tm, LANES), lambda i: (i, 0)),
                  pl.BlockSpec((1, d), lambda i: (0, 0)),
                  pl.BlockSpec(memory_space=pl.ANY)],
        out_specs=pl.BlockSpec((tm, d), lambda i: (i, 0)),
        out_shape=jax.ShapeDtypeStruct((n, d), F32),
        scratch_shapes=[pltpu.VMEM((TOP_K, tm, d), F32), pltpu.SemaphoreType.DMA((2,))],
        compiler_params=pltpu.CompilerParams(
            dimension_semantics=("arbitrary",), vmem_limit_bytes=VMEM_LIMIT),
        name="moe_combine_norm",
    )(dest_flat, x2, route, g_final, ys)


def _layer(x, mem, g_mix, w_in, conv_w, g_conv_out, g_sb_out, w_out, g_xattn, g_mem,
           w_q_mem, w_kv_mem, w_o_mem, g_moe, w_router, b_router, w1, b1, w2, b2):
    b, s, d = x.shape
    n = b * s
    row2 = lambda g: g.reshape(1, -1)

    grp = jnp.arange(CONV_DIM) // CONV_GROUP_DIM
    gmat = jnp.where(grp[:, None] == grp[None, :], 1.0 / CONV_GROUP_DIM, 0.0).astype(BF16)
    ar = jnp.arange(SB_TILE)
    tri = (ar[:, None] > ar[None, :]).astype(BF16)
    tri = jnp.concatenate([tri, tri], axis=0)
    ar = jnp.arange(ROW_TILE)
    ltri = (ar[:, None] > ar[None, :]).astype(BF16)

    kv = _mem_kv(mem, row2(g_mem), w_kv_mem.astype(BF16))
    yc, q, k, v = _inproj(x, row2(g_mix), w_in.astype(BF16), conv_w, row2(g_conv_out), gmat)
    ys = _stick_breaking(q, k, v, tri, row2(g_sb_out))
    w_out_b = w_out.astype(BF16)
    wr_hi = w_router.astype(BF16)
    wr_lo = (w_router - wr_hi.astype(F32)).astype(BF16)
    x2, h3, route, counts = _mid(x, yc, ys, w_out_b, w_out_b, row2(g_xattn),
                                 w_q_mem.astype(BF16), kv, w_o_mem.astype(BF16), row2(g_moe),
                                 jnp.concatenate([wr_hi, wr_hi, wr_lo], axis=0), row2(b_router), ltri)

    te = EXPERT_TILE
    route = route.reshape(n, LANES)
    counts = counts.reshape(N_EXPERTS).astype(jnp.int32)
    padded = (counts + te - 1) // te * te
    padded_end = jnp.cumsum(padded)
    start = padded_end - padded
    top_i = route[:, 0:TOP_K].astype(jnp.int32)
    rank = route[:, 2 * TOP_K:3 * TOP_K].astype(jnp.int32)
    experts = jnp.arange(N_EXPERTS, dtype=jnp.int32)
    first = jnp.sum(jnp.where(top_i[:, :, None] == experts, start, 0), axis=-1)
    dest = (first + rank).reshape(n * TOP_K)
    n_rows = n * TOP_K + N_EXPERTS * te
    n_blocks = n_rows // te
    n_used = (padded_end[-1] // te).astype(jnp.int32)
    blk = jnp.minimum(jnp.arange(n_blocks, dtype=jnp.int32), n_used - 1) * te
    block_e = jnp.minimum(jnp.sum((padded_end[None, :] <= blk[:, None]).astype(jnp.int32), axis=1),
                          N_EXPERTS - 1)

    xs = _dispatch(counts, padded, start, dest, h3.reshape(n, d), n_rows)
    block_valid = jnp.sum(jnp.where(block_e[:, None] == experts, counts - (blk[:, None] - start), 0), axis=1)
    nonempty = counts > 0
    slot_e = (jnp.cumsum(nonempty.astype(jnp.int32)) - 1) % 2
    later = (experts[None, :] > experts[:, None]) & nonempty[None, :]
    next_e = jnp.min(jnp.where(later, experts[None, :], N_EXPERTS), axis=1)
    next_e = jnp.where(next_e < N_EXPERTS, next_e, -1)
    of_block = block_e[:, None] == experts
    block_slot = jnp.sum(jnp.where(of_block, slot_e, 0), axis=1).astype(jnp.int32)
    block_next = jnp.sum(jnp.where(of_block, next_e, 0), axis=1).astype(jnp.int32)
    yo = _experts(block_e, n_used.reshape(1), block_valid.astype(jnp.int32), block_slot, block_next,
                  xs, w1, b1, w2, b2)
    return dest, x2.reshape(n, d), route, yo


def kernel(x, mem, g_mix, w_in, conv_w, g_conv_out, g_sb_out, w_out, g_xattn, g_mem, w_q_mem,
           w_kv_mem, w_o_mem, g_moe, w_router, b_router, w1, b1, w2, b2, g_final):
    b, s, d = x.shape
    assert w_in.shape[0] == 1, "single-layer stack"
    dest, x2, route, yo = _layer(x, mem, g_mix[0], w_in[0], conv_w[0], g_conv_out[0], g_sb_out[0],
                                 w_out[0], g_xattn[0], g_mem[0], w_q_mem[0], w_kv_mem[0],
                                 w_o_mem[0], g_moe[0], w_router[0], b_router[0],
                                 w1[0], b1[0], w2[0], b2[0])
    out = _combine(dest, x2, route, g_final.reshape(1, d), yo)
    return out.reshape(b, s, d)
```

```python
import jax
import jax.numpy as jnp
from jax import lax
from jax.experimental import pallas as pl
from jax.experimental.pallas import tpu as pltpu

F32 = jnp.float32
BF16 = jnp.bfloat16

EPS = 1e-5
LOG2E = 1.4426950408889634
CONV_DIM = 512
SB_DIM = 512
SB_HEAD_DIM = 64
CONV_GROUP_DIM = 64
X_HEADS = 4
N_EXPERTS = 32
TOP_K = 4
SWIGLU_LIMIT = 7.0
SWIGLU_ALPHA = 1.702

LANES = 128
ROW_TILE = 512
SB_TILE = 256
EXPERT_TILE = 512
EXPERT_CHUNK = 256
VMEM_LIMIT = 56 * 1024 * 1024


def _rms(x, g):
    return x * lax.rsqrt(jnp.mean(x * x, axis=-1, keepdims=True) + EPS) * g


def _split_dot(a, m):
    hi = a.astype(BF16)
    lo = (a - hi.astype(F32)).astype(BF16)
    return (jnp.dot(hi, m, preferred_element_type=F32)
            + jnp.dot(lo, m, preferred_element_type=F32))


def _mem_kv_kernel(mem_ref, g_ref, w_ref, kv_ref):
    h = _rms(mem_ref[0], g_ref[...]).astype(BF16)
    kv_ref[0] = jnp.dot(h, w_ref[...], preferred_element_type=F32).astype(BF16)


def _mem_kv(mem, g_mem, w_kv):
    b, m, d = mem.shape
    return pl.pallas_call(
        _mem_kv_kernel,
        grid=(b,),
        in_specs=[pl.BlockSpec((1, m, d), lambda i: (i, 0, 0)),
                  pl.BlockSpec((1, d), lambda i: (0, 0)),
                  pl.BlockSpec((d, 2 * d), lambda i: (0, 0))],
        out_specs=pl.BlockSpec((1, m, 2 * d), lambda i: (i, 0, 0)),
        out_shape=jax.ShapeDtypeStruct((b, m, 2 * d), BF16),
        compiler_params=pltpu.CompilerParams(
            dimension_semantics=("arbitrary",), vmem_limit_bytes=VMEM_LIMIT),
        name="mem_kv",
    )(mem, g_mem, w_kv)


def _inproj_kernel(x_ref, g_ref, w_ref, cw_ref, gc_ref, gmat_ref,
                   yc_ref, q_ref, k_ref, v_ref, prev_ref):
    tm = x_ref.shape[1]
    h = _rms(x_ref[0], g_ref[...]).astype(BF16)
    proj = jnp.dot(h, w_ref[...], preferred_element_type=F32)
    c = CONV_DIM
    u = proj[:, c:2 * c] * proj[:, 2 * c:3 * c]

    @pl.when(pl.program_id(1) == 0)
    def _():
        prev_ref[...] = jnp.zeros_like(prev_ref)

    rows = lax.broadcasted_iota(jnp.int32, (tm, 1), 0)
    p1 = prev_ref[7:8, :]
    p2 = prev_ref[6:7, :]
    u1 = jnp.where(rows == 0, p1, pltpu.roll(u, 1, axis=0))
    u2 = jnp.where(rows == 0, p2, jnp.where(rows == 1, p1, pltpu.roll(u, 2, axis=0)))
    prev_ref[...] = u[tm - 8:tm, :]
    y = proj[:, 0:c] * (cw_ref[0:1, :] * u2 + cw_ref[1:2, :] * u1 + cw_ref[2:3, :] * u)
    ms = _split_dot(y * y, gmat_ref[...])
    yc_ref[0] = (y * lax.rsqrt(ms + EPS) * gc_ref[...]).astype(BF16)
    o = 3 * c
    q_ref[0] = (proj[:, o:o + SB_DIM] * (SB_HEAD_DIM ** -0.5)).astype(BF16)
    k_ref[0] = proj[:, o + SB_DIM:o + 2 * SB_DIM].astype(BF16)
    v_ref[0] = proj[:, o + 2 * SB_DIM:o + 3 * SB_DIM].astype(BF16)


def _inproj(x, g_mix, w_in, conv_w, g_conv, gmat):
    b, s, d = x.shape
    tm = ROW_TILE
    p = w_in.shape[1]
    row = lambda i, j: (i, j, 0)
    fixed = lambda i, j: (0, 0)
    outs = [jax.ShapeDtypeStruct((b, s, CONV_DIM), BF16)] * 4
    return pl.pallas_call(
        _inproj_kernel,
        grid=(b, s // tm),
        in_specs=[pl.BlockSpec((1, tm, d), row),
                  pl.BlockSpec((1, d), fixed),
                  pl.BlockSpec((d, p), fixed),
                  pl.BlockSpec(conv_w.shape, fixed),
                  pl.BlockSpec((1, CONV_DIM), fixed),
                  pl.BlockSpec((CONV_DIM, CONV_DIM), fixed)],
        out_specs=[pl.BlockSpec((1, tm, CONV_DIM), row)] * 4,
        out_shape=outs,
        scratch_shapes=[pltpu.VMEM((8, CONV_DIM), F32)],
        compiler_params=pltpu.CompilerParams(
            dimension_semantics=("arbitrary", "arbitrary"), vmem_limit_bytes=VMEM_LIMIT),
        name="inproj_conv",
    )(x, g_mix, w_in, conv_w, g_conv, gmat)


def _sb_kernel(q_ref, k_ref, v_ref, tri_ref, g_ref, o_ref):
    tq = tri_ref.shape[1]
    tk = tq
    lane = lax.broadcasted_iota(jnp.int32, (1, LANES), 1)
    in_a = lane < SB_HEAD_DIM
    tri = tri_ref[...]

    def scores(qh, kb):
        return lax.dot_general(qh, kb, (((1,), (1,)), ((), ())), preferred_element_type=F32) * LOG2E

    def later_keys(z2, valid):
        sp2 = jnp.maximum(z2, 0.0) + jnp.log2(1.0 + jnp.exp2(-jnp.abs(z2)))
        if valid is not None:
            sp2 = jnp.where(valid, sp2, 0.0)
        hi = sp2.astype(BF16)
        lo = (sp2 - hi.astype(F32)).astype(BF16)
        rest = jnp.dot(jnp.concatenate([hi, lo], axis=1), tri,
                       preferred_element_type=F32)
        return sp2, rest

    def weights(z2, sp2, rest, run, valid):
        a = jnp.exp2(z2 - sp2 - rest - run)
        if valid is not None:
            a = jnp.where(valid, a, 0.0)
        return a.astype(BF16), run + rest[:, 0:1] + sp2[:, 0:1]

    def tiles(q_heads, js, carry, valids):
        run = [carry[0], carry[2]]
        acc = [carry[1], carry[3]]
        kbs, vbs = [], []
        for j in js:
            start = pl.multiple_of(j * tk, tk)
            kbs.append(k_ref[0, pl.ds(start, tk), :])
            vbs.append(v_ref[0, pl.ds(start, tk), :])
        zs = [[scores(qh, kb) for qh in q_heads] for kb in kbs]
        mids = [[later_keys(z, valid) for z in zt] for zt, valid in zip(zs, valids)]
        for t in range(len(js)):
            for h in range(2):
                w, run[h] = weights(zs[t][h], *mids[t][h], run[h], valids[t])
                acc[h] = acc[h] + jnp.dot(w, vbs[t], preferred_element_type=F32)
        return run[0], acc[0], run[1], acc[1]

    r0 = jnp.zeros((tq, 1), F32)
    a0 = jnp.zeros((tq, LANES), F32)
    t_idx = lax.broadcasted_iota(jnp.int32, (tq, tk), 0)
    s_idx = lax.broadcasted_iota(jnp.int32, (tq, tk), 1)

    def query_tile(qi, _):
        rows = pl.ds(pl.multiple_of(qi * tq, tq), tq)
        q = q_ref[0, rows, :]
        zero = jnp.zeros_like(q)
        q_heads = (jnp.where(in_a, q, zero), jnp.where(in_a, zero, q))
        diag = s_idx < t_idx
        start = (r0, a0, r0, a0)
        odd = qi % 2
        carry = lax.cond(odd == 1,
                         lambda c: tiles(q_heads, [qi, qi - 1], c, [diag, None]),
                         lambda c: tiles(q_heads, [qi], c, [diag]), start)
        first = qi - 1 - odd
        n_pairs = qi // 2
        carry = lax.cond(n_pairs % 2 == 1,
                         lambda c: tiles(q_heads, [first, first - 1], c, [None, None]),
                         lambda c: c, carry)
        first = first - 2 * (n_pairs % 2)
        carry = lax.fori_loop(
            0, n_pairs // 2,
            lambda n, c: tiles(q_heads, [first - 4 * n - u for u in range(4)], c, [None] * 4), carry)
        out = jnp.where(in_a, carry[1], carry[3])
        sq = out * out
        ss_a = jnp.sum(jnp.where(in_a, sq, 0.0), axis=-1, keepdims=True)
        ss_b = jnp.sum(jnp.where(in_a, 0.0, sq), axis=-1, keepdims=True)
        ms = jnp.where(in_a, ss_a, ss_b) * (1.0 / SB_HEAD_DIM)
        o_ref[0, rows, :] = (out * lax.rsqrt(ms + EPS) * g_ref[...]).astype(BF16)
        return 0

    lax.fori_loop(0, q_ref.shape[1] // tq, query_tile, 0)


def _stick_breaking(q, k, v, tri, g_sb):
    b, s, w = q.shape
    tq = SB_TILE
    return pl.pallas_call(
        _sb_kernel,
        grid=(b, w // LANES),
        in_specs=[pl.BlockSpec((1, s, LANES), lambda i, h: (i, 0, h)),
                  pl.BlockSpec((1, s, LANES), lambda i, h: (i, 0, h)),
                  pl.BlockSpec((1, s, LANES), lambda i, h: (i, 0, h)),
                  pl.BlockSpec((2 * tq, tq), lambda i, h: (0, 0)),
                  pl.BlockSpec((1, LANES), lambda i, h: (0, h))],
        out_specs=pl.BlockSpec((1, s, LANES), lambda i, h: (i, 0, h)),
        out_shape=jax.ShapeDtypeStruct((b, s, w), BF16),
        compiler_params=pltpu.CompilerParams(
            dimension_semantics=("arbitrary", "arbitrary"),
            vmem_limit_bytes=VMEM_LIMIT),
        name="stick_breaking",
    )(q, k, v, tri, g_sb)


def _mid_kernel(x_ref, yc_ref, ys_ref, wo1_ref, wo2_ref, gx_ref, wq_ref, kv_ref, wom_ref,
                gm_ref, wr_ref, br_ref, ltri_ref,
                x2_ref, h3_ref, route_ref, cnt_ref, carry_ref):
    tm = x_ref.shape[1]
    d = x_ref.shape[2]
    hd = d // X_HEADS

    @pl.when((pl.program_id(0) == 0) & (pl.program_id(1) == 0))
    def _():
        carry_ref[...] = jnp.zeros_like(carry_ref)

    x1 = (x_ref[0]
          + jnp.dot(yc_ref[0], wo1_ref[...], preferred_element_type=F32)
          + jnp.dot(ys_ref[0], wo2_ref[...], preferred_element_type=F32))
    h2 = _rms(x1, gx_ref[...]).astype(BF16)
    qm = (jnp.dot(h2, wq_ref[...], preferred_element_type=F32) * (hd ** -0.5)).astype(BF16)
    heads = []
    for h in range(X_HEADS):
        kh = kv_ref[0, :, h * hd:(h + 1) * hd]
        vh = kv_ref[0, :, d + h * hd:d + (h + 1) * hd]
        sc = lax.dot_general(qm[:, h * hd:(h + 1) * hd], kh, (((1,), (1,)), ((), ())),
                             preferred_element_type=F32)
        e = jnp.exp(sc - jnp.max(sc, axis=-1, keepdims=True))
        p = e / jnp.sum(e, axis=-1, keepdims=True)
        heads.append(jnp.dot(p.astype(BF16), vh, preferred_element_type=F32).astype(BF16))
    x2 = x1 + jnp.dot(jnp.concatenate(heads, axis=-1), wom_ref[...], preferred_element_type=F32)
    x2_ref[0] = x2
    h3 = _rms(x2, gm_ref[...])
    h3_ref[0] = h3

    h_hi = h3.astype(BF16)
    h_lo = (h3 - h_hi.astype(F32)).astype(BF16)
    logits = jnp.dot(jnp.concatenate([h_hi, h_lo, h_hi], axis=1), wr_ref[...],
                     preferred_element_type=F32) + br_ref[...]
    lane = lax.broadcasted_iota(jnp.int32, (tm, N_EXPERTS), 1).astype(F32)
    vals, idxs = [], []
    l = logits
    for _ in range(TOP_K):
        m = jnp.max(l, axis=-1, keepdims=True)
        idx = jnp.min(jnp.where(l == m, lane, float(N_EXPERTS)), axis=-1, keepdims=True)
        vals.append(m)
        idxs.append(idx)
        l = jnp.where(lane == idx, -jnp.inf, l)
    es = [jnp.exp(vk - vals[0]) for vk in vals]
    den = es[0] + es[1] + es[2] + es[3]
    gates = [ek / den for ek in es]

    sel = jnp.zeros((tm, N_EXPERTS), F32)
    for idx in idxs:
        sel = sel + (lane == idx).astype(F32)
    before = jnp.dot(ltri_ref[...], sel.astype(BF16), preferred_element_type=F32) + carry_ref[...]
    ranks = [jnp.sum(jnp.where(lane == idx, before, 0.0), axis=-1, keepdims=True) for idx in idxs]
    carry_ref[...] += jnp.sum(sel, axis=0, keepdims=True)
    cnt_ref[...] = carry_ref[...]

    lane_o = lax.broadcasted_iota(jnp.int32, (tm, LANES), 1)
    r = jnp.zeros((tm, LANES), F32)
    for k in range(TOP_K):
        r = jnp.where(lane_o == k, idxs[k], r)
        r = jnp.where(lane_o == TOP_K + k, gates[k], r)
        r = jnp.where(lane_o == 2 * TOP_K + k, ranks[k], r)
    route_ref[0] = r


def _mid(x, yc, ys, wo1, wo2, g_x, w_q, kv, w_om, g_moe, w_r, b_r, ltri):
    b, s, d = x.shape
    tm = ROW_TILE
    m = kv.shape[1]
    row = lambda i, j: (i, j, 0)
    fixed = lambda i, j: (0, 0)
    return pl.pallas_call(
        _mid_kernel,
        grid=(b, s // tm),
        in_specs=[pl.BlockSpec((1, tm, d), row),
                  pl.BlockSpec((1, tm, CONV_DIM), row),
                  pl.BlockSpec((1, tm, SB_DIM), row),
                  pl.BlockSpec((CONV_DIM, d), fixed),
                  pl.BlockSpec((SB_DIM, d), lambda i, j: (CONV_DIM // SB_DIM, 0)),
                  pl.BlockSpec((1, d), fixed),
                  pl.BlockSpec((d, d), fixed),
                  pl.BlockSpec((1, m, 2 * d), lambda i, j: (i, 0, 0)),
                  pl.BlockSpec((d, d), fixed),
                  pl.BlockSpec((1, d), fixed),
                  pl.BlockSpec((3 * d, N_EXPERTS), fixed),
                  pl.BlockSpec((1, N_EXPERTS), fixed),
                  pl.BlockSpec((tm, tm), fixed)],
        out_specs=[pl.BlockSpec((1, tm, d), row),
                   pl.BlockSpec((1, tm, d), row),
                   pl.BlockSpec((1, tm, LANES), row),
                   pl.BlockSpec((1, N_EXPERTS), fixed)],
        out_shape=[jax.ShapeDtypeStruct((b, s, d), F32),
                   jax.ShapeDtypeStruct((b, s, d), F32),
                   jax.ShapeDtypeStruct((b, s, LANES), F32),
                   jax.ShapeDtypeStruct((1, N_EXPERTS), F32)],
        scratch_shapes=[pltpu.VMEM((1, N_EXPERTS), F32)],
        compiler_params=pltpu.CompilerParams(
            dimension_semantics=("arbitrary", "arbitrary"), vmem_limit_bytes=VMEM_LIMIT),
        name="outproj_xattn_router",
    )(x, yc, ys, wo1, wo2, g_x, w_q, kv, w_om, g_moe, w_r, b_r, ltri)


def _row_copy(src, src_row, dst, dst_row, sem):
    return pltpu.make_async_copy(src.at[pl.ds(src_row, 1), :], dst.at[pl.ds(dst_row, 1), :], sem)


def _dispatch_kernel(cnt_ref, pad_ref, start_ref, dest_ref, h_ref, zero_ref, xs_ref, stage_ref, sem, zsem):
    i = pl.program_id(0)
    n_steps = pl.num_programs(0)
    tm = h_ref.shape[0]
    te = zero_ref.shape[0]
    slot = i % 2

    def block_copy(row):
        rows = pl.ds(pl.multiple_of(row, te), te)
        return pltpu.make_async_copy(zero_ref, xs_ref.at[rows, :], zsem)

    @pl.when(i == 0)
    def _():
        last = N_EXPERTS - 1
        n_used = (start_ref[last] + pad_ref[last]) // te
        n_blocks = xs_ref.shape[0] // te

        def fill(wait):
            def expert(e, c):
                @pl.when(pad_ref[e] > cnt_ref[e])
                def _():
                    cp = block_copy(start_ref[e] + pad_ref[e] - te)
                    cp.wait() if wait else cp.start()
                return c

            def tail(j, c):
                cp = block_copy(j * te)
                cp.wait() if wait else cp.start()
                return c

            lax.fori_loop(0, N_EXPERTS, expert, 0)
            lax.fori_loop(n_used, n_blocks, tail, 0)

        fill(False)
        fill(True)

    def drain(s):
        def body(t, c):
            for k in range(TOP_K):
                _row_copy(stage_ref.at[s], t, xs_ref, 0, sem.at[s]).wait()
            return c
        lax.fori_loop(0, tm, body, 0)

    @pl.when(i >= 2)
    def _():
        drain(slot)

    stage_ref[slot] = h_ref[...]

    def issue(t, c):
        for k in range(TOP_K):
            _row_copy(stage_ref.at[slot], t, xs_ref, dest_ref[t * TOP_K + k],
                      sem.at[slot]).start(priority=k % 2)
        return c

    lax.fori_loop(0, tm, issue, 0, unroll=8)

    @pl.when(i == n_steps - 1)
    def _():
        @pl.when(n_steps > 1)
        def _():
            drain(1 - slot)
        drain(slot)


def _dispatch(counts, padded, start, dest_flat, h3, n_rows):
    n, d = h3.shape
    tm = ROW_TILE
    zero = jnp.zeros((EXPERT_TILE, d), h3.dtype)
    return pl.pallas_call(
        _dispatch_kernel,
        grid_spec=pltpu.PrefetchScalarGridSpec(
            num_scalar_prefetch=3,
            grid=(n // tm,),
            in_specs=[pl.BlockSpec((tm * TOP_K,), lambda i, *_: (i,), memory_space=pltpu.SMEM),
                      pl.BlockSpec((tm, d), lambda i, *_: (i, 0)),
                      pl.BlockSpec(zero.shape, lambda i, *_: (0, 0))],
            out_specs=pl.BlockSpec(memory_space=pl.ANY),
            scratch_shapes=[pltpu.VMEM((2, tm, d), h3.dtype), pltpu.SemaphoreType.DMA((2,)),
                            pltpu.SemaphoreType.DMA]),
        out_shape=jax.ShapeDtypeStruct((n_rows, d), h3.dtype),
        compiler_params=pltpu.CompilerParams(
            dimension_semantics=("arbitrary",), vmem_limit_bytes=VMEM_LIMIT),
        name="moe_dispatch",
    )(counts, padded, start, dest_flat, h3, zero)


def _expert_kernel(be_ref, nu_ref, bv_ref, slot_ref, nxt_ref, xs_ref, w1_ref, b1_ref, w2_ref, b2_ref, o_ref,
                   w1c_ref, w2c_ref, w1f_ref, w2f_ref, wsem):
    i = pl.program_id(0)
    tm = xs_ref.shape[0]
    de = w2_ref.shape[1]

    def weight_copies(e, s):
        return (pltpu.make_async_copy(w1_ref.at[e], w1f_ref.at[s], wsem.at[s, 0]),
                pltpu.make_async_copy(w2_ref.at[e], w2f_ref.at[s], wsem.at[s, 1]))

    @pl.when(i >= nu_ref[0])
    def _():
        o_ref[...] = jnp.zeros_like(o_ref)

    @pl.when(i < nu_ref[0])
    def _():
        changed = (i == 0) | (be_ref[i] != be_ref[jnp.maximum(i - 1, 0)])

        @pl.when(changed)
        def _():
            e = be_ref[i]
            s = slot_ref[i]

            @pl.when(i == 0)
            def _():
                for cp in weight_copies(e, s):
                    cp.start()

            for cp in weight_copies(e, s):
                cp.wait()
            w1c_ref[...] = w1f_ref[s].astype(BF16)
            w2c_ref[...] = w2f_ref[s].astype(BF16)

            @pl.when(nxt_ref[i] >= 0)
            def _():
                for cp in weight_copies(nxt_ref[i], 1 - s):
                    cp.start()

        ch = EXPERT_CHUNK

        def activation(xb, c):
            g = (jnp.dot(xb, w1c_ref[:, c * ch:(c + 1) * ch], preferred_element_type=F32)
                 + b1_ref[0, :, c * ch:(c + 1) * ch])
            l = (jnp.dot(xb, w1c_ref[:, de + c * ch:de + (c + 1) * ch], preferred_element_type=F32)
                 + b1_ref[0, :, de + c * ch:de + (c + 1) * ch])
            glu = jnp.minimum(g, SWIGLU_LIMIT)
            lin = jnp.clip(l, -SWIGLU_LIMIT, SWIGLU_LIMIT)
            half = 0.5 * glu
            return ((half + half * jnp.tanh((0.5 * SWIGLU_ALPHA) * glu)) * (lin + 1.0)).astype(BF16)

        def mlp(rows):
            xb = xs_ref[0:rows, :].astype(BF16)
            out = b2_ref[0]
            act = activation(xb, 0)
            for c in range(de // ch):
                nxt = activation(xb, c + 1) if (c + 1) * ch < de else None
                out = out + jnp.dot(act, w2c_ref[c * ch:(c + 1) * ch, :], preferred_element_type=F32)
                act = nxt
            o_ref[0:rows, :] = out
            if rows < tm:
                o_ref[rows:tm, :] = jnp.zeros((tm - rows, o_ref.shape[1]), F32)

        @pl.when(bv_ref[i] > tm // 2)
        def _():
            mlp(tm)

        @pl.when(bv_ref[i] <= tm // 2)
        def _():
            mlp(tm // 2)


def _experts(block_e, n_used, block_valid, block_slot, block_next, xs, w1, b1, w2, b2):
    n_rows, dp = xs.shape
    tm = EXPERT_TILE
    ne, d, dh = w1.shape
    de = w2.shape[1]
    rows = lambda i, be, nu, *_: (jnp.maximum(jnp.minimum(i, nu[0] - 1), 0), 0)
    per_e = lambda i, be, *_: (be[i], 0, 0)
    return pl.pallas_call(
        _expert_kernel,
        grid_spec=pltpu.PrefetchScalarGridSpec(
            num_scalar_prefetch=5,
            grid=(n_rows // tm,),
            in_specs=[pl.BlockSpec((tm, dp), rows),
                      pl.BlockSpec(memory_space=pl.ANY),
                      pl.BlockSpec((1, 1, dh), per_e),
                      pl.BlockSpec(memory_space=pl.ANY),
                      pl.BlockSpec((1, 1, d), per_e)],
            out_specs=pl.BlockSpec((tm, d), lambda i, *_: (i, 0)),
            scratch_shapes=[pltpu.VMEM((d, dh), BF16), pltpu.VMEM((de, d), BF16),
                            pltpu.VMEM((2, d, dh), F32), pltpu.VMEM((2, de, d), F32),
                            pltpu.SemaphoreType.DMA((2, 2))]),
        out_shape=jax.ShapeDtypeStruct((n_rows, d), F32),
        compiler_params=pltpu.CompilerParams(
            dimension_semantics=("arbitrary",), vmem_limit_bytes=VMEM_LIMIT),
        name="moe_experts",
    )(block_e, n_used, block_valid, block_slot, block_next, xs, w1, b1.reshape(ne, 1, dh), w2,
      b2.reshape(ne, 1, d))


def _combine_kernel(dest_ref, x2_ref, route_ref, g_ref, ys_ref, o_ref, buf_ref, sem):
    tm = x2_ref.shape[0]
    half = tm // 2

    def gather(r0, s, wait):
        def body(t, c):
            for k in range(TOP_K):
                cp = _row_copy(ys_ref, dest_ref[t * TOP_K + k], buf_ref.at[k], t, sem.at[s])
                cp.wait() if wait else cp.start(priority=k % 2)
            return c
        if wait:
            lax.fori_loop(r0, r0 + half, body, 0)
        else:
            lax.fori_loop(r0, r0 + half, body, 0, unroll=8)

    def finish(r0):
        route = route_ref[r0:r0 + half, :]
        x3 = x2_ref[r0:r0 + half, :]
        for k in range(TOP_K):
            x3 = x3 + buf_ref[k, r0:r0 + half, :] * route[:, TOP_K + k:TOP_K + k + 1]
        o_ref[r0:r0 + half, :] = _rms(x3, g_ref[...])

    gather(0, 0, False)
    gather(half, 1, False)
    gather(0, 0, True)
    finish(0)
    gather(half, 1, True)
    finish(half)


def _combine(dest_flat, x2, route, g_final, ys):
    n, d = x2.shape
    tm = ROW_TILE
    return pl.pallas_call(
        _combine_kernel,
        grid=(n // tm,),
        in_specs=[pl.BlockSpec((tm * TOP_K,), lambda i: (i,), memory_space=pltpu.SMEM),
                  pl.BlockSpec((tm, d), lambda i: (i, 0)),
                  pl.BlockSpec((tm, LANES), lambda i: (i, 0)),
                  pl.BlockSpec((1, d), lambda i: (0, 0)),
                  pl.BlockSpec(memory_space=pl.ANY)],
        out_specs=pl.BlockSpec((tm, d), lambda i: (i, 0)),
        out_shape=jax.ShapeDtypeStruct((n, d), F32),
        scratch_shapes=[pltpu.VMEM((TOP_K, tm, d), F32), pltpu.SemaphoreType.DMA((2,))],
        compiler_params=pltpu.CompilerParams(
            dimension_semantics=("arbitrary",), vmem_limit_bytes=VMEM_LIMIT),
        name="moe_combine_norm",
    )(dest_flat, x2, route, g_final, ys)


def _layer(x, mem, g_mix, w_in, conv_w, g_conv_out, g_sb_out, w_out, g_xattn, g_mem,
           w_q_mem, w_kv_mem, w_o_mem, g_moe, w_router, b_router, w1, b1, w2, b2):
    b, s, d = x.shape
    n = b * s
    row2 = lambda g: g.reshape(1, -1)

    grp = jnp.arange(CONV_DIM) // CONV_GROUP_DIM
    gmat = jnp.where(grp[:, None] == grp[None, :], 1.0 / CONV_GROUP_DIM, 0.0).astype(BF16)
    ar = jnp.arange(SB_TILE)
    tri = (ar[:, None] > ar[None, :]).astype(BF16)
    tri = jnp.concatenate([tri, tri], axis=0)
    ar = jnp.arange(ROW_TILE)
    ltri = (ar[:, None] > ar[None, :]).astype(BF16)

    kv = _mem_kv(mem, row2(g_mem), w_kv_mem.astype(BF16))
    yc, q, k, v = _inproj(x, row2(g_mix), w_in.astype(BF16), conv_w, row2(g_conv_out), gmat)
    ys = _stick_breaking(q, k, v, tri, row2(g_sb_out))
    w_out_b = w_out.astype(BF16)
    wr_hi = w_router.astype(BF16)
    wr_lo = (w_router - wr_hi.astype(F32)).astype(BF16)
    x2, h3, route, counts = _mid(x, yc, ys, w_out_b, w_out_b, row2(g_xattn),
                                 w_q_mem.astype(BF16), kv, w_o_mem.astype(BF16), row2(g_moe),
                                 jnp.concatenate([wr_hi, wr_hi, wr_lo], axis=0), row2(b_router), ltri)

    te = EXPERT_TILE
    route = route.reshape(n, LANES)
    counts = counts.reshape(N_EXPERTS).astype(jnp.int32)
    padded = (counts + te - 1) // te * te
    padded_end = jnp.cumsum(padded)
    start = padded_end - padded
    top_i = route[:, 0:TOP_K].astype(jnp.int32)
    rank = route[:, 2 * TOP_K:3 * TOP_K].astype(jnp.int32)
    experts = jnp.arange(N_EXPERTS, dtype=jnp.int32)
    first = jnp.sum(jnp.where(top_i[:, :, None] == experts, start, 0), axis=-1)
    dest = (first + rank).reshape(n * TOP_K)
    n_rows = n * TOP_K + N_EXPERTS * te
    n_blocks = n_rows // te
    n_used = (padded_end[-1] // te).astype(jnp.int32)
    blk = jnp.minimum(jnp.arange(n_blocks, dtype=jnp.int32), n_used - 1) * te
    block_e = jnp.minimum(jnp.sum((padded_end[None, :] <= blk[:, None]).astype(jnp.int32), axis=1),
                          N_EXPERTS - 1)

    xs = _dispatch(counts, padded, start, dest, h3.reshape(n, d), n_rows)
    block_valid = jnp.sum(jnp.where(block_e[:, None] == experts, counts - (blk[:, None] - start), 0), axis=1)
    nonempty = counts > 0
    slot_e = (jnp.cumsum(nonempty.astype(jnp.int32)) - 1) % 2
    later = (experts[None, :] > experts[:, None]) & nonempty[None, :]
    next_e = jnp.min(jnp.where(later, experts[None, :], N_EXPERTS), axis=1)
    next_e = jnp.where(next_e < N_EXPERTS, next_e, -1)
    of_block = block_e[:, None] == experts
    block_slot = jnp.sum(jnp.where(of_block, slot_e, 0), axis=1).astype(jnp.int32)
    block_next = jnp.sum(jnp.where(of_block, next_e, 0), axis=1).astype(jnp.int32)
    yo = _experts(block_e, n_used.reshape(1), block_valid.astype(jnp.int32), block_slot, block_next,
                  xs, w1, b1, w2, b2)
    return dest, x2.reshape(n, d), route, yo


def kernel(x, mem, g_mix, w_in, conv_w, g_conv_out, g_sb_out, w_out, g_xattn, g_mem, w_q_mem,
           w_kv_mem, w_o_mem, g_moe, w_router, b_router, w1, b1, w2, b2, g_final):
    b, s, d = x.shape
    assert w_in.shape[0] == 1, "single-layer stack"
    dest, x2, route, yo = _layer(x, mem, g_mix[0], w_in[0], conv_w[0], g_conv_out[0], g_sb_out[0],
                                 w_out[0], g_xattn[0], g_mem[0], w_q_mem[0], w_kv_mem[0],
                                 w_o_mem[0], g_moe[0], w_router[0], b_router[0],
                                 w1[0], b1[0], w2[0], b2[0])
    out = _combine(dest, x2, route, g_final.reshape(1, d), yo)
    return out.reshape(b, s, d)
```

```python
import jax
import jax.numpy as jnp
from jax import lax
from jax.experimental import pallas as pl
from jax.experimental.pallas import tpu as pltpu

F32 = jnp.float32
BF16 = jnp.bfloat16

EPS = 1e-5
LOG2E = 1.4426950408889634
CONV_DIM = 512
SB_DIM = 512
SB_HEAD_DIM = 64
CONV_GROUP_DIM = 64
X_HEADS = 4
N_EXPERTS = 32
TOP_K = 4
SWIGLU_LIMIT = 7.0
SWIGLU_ALPHA = 1.702

LANES = 128
ROW_TILE = 512
SB_TILE = 256
EXPERT_TILE = 512
EXPERT_CHUNK = 256
WAIT_ROWS = 8
VMEM_LIMIT = 56 * 1024 * 1024


def _rms(x, g):
    return x * lax.rsqrt(jnp.mean(x * x, axis=-1, keepdims=True) + EPS) * g


def _split_dot(a, m):
    hi = a.astype(BF16)
    lo = (a - hi.astype(F32)).astype(BF16)
    return (jnp.dot(hi, m, preferred_element_type=F32)
            + jnp.dot(lo, m, preferred_element_type=F32))


def _mem_kv_kernel(mem_ref, g_ref, w_ref, kv_ref):
    h = _rms(mem_ref[0], g_ref[...]).astype(BF16)
    kv_ref[0] = jnp.dot(h, w_ref[...], preferred_element_type=F32).astype(BF16)


def _mem_kv(mem, g_mem, w_kv):
    b, m, d = mem.shape
    return pl.pallas_call(
        _mem_kv_kernel,
        grid=(b,),
        in_specs=[pl.BlockSpec((1, m, d), lambda i: (i, 0, 0)),
                  pl.BlockSpec((1, d), lambda i: (0, 0)),
                  pl.BlockSpec((d, 2 * d), lambda i: (0, 0))],
        out_specs=pl.BlockSpec((1, m, 2 * d), lambda i: (i, 0, 0)),
        out_shape=jax.ShapeDtypeStruct((b, m, 2 * d), BF16),
        compiler_params=pltpu.CompilerParams(
            dimension_semantics=("arbitrary",), vmem_limit_bytes=VMEM_LIMIT),
        name="mem_kv",
    )(mem, g_mem, w_kv)


def _inproj_kernel(x_ref, g_ref, w_ref, cw_ref, gc_ref, gmat_ref,
                   yc_ref, q_ref, k_ref, v_ref, prev_ref):
    tm = x_ref.shape[1]
    h = _rms(x_ref[0], g_ref[...]).astype(BF16)
    proj = jnp.dot(h, w_ref[...], preferred_element_type=F32)
    c = CONV_DIM
    u = proj[:, c:2 * c] * proj[:, 2 * c:3 * c]

    @pl.when(pl.program_id(1) == 0)
    def _():
        prev_ref[...] = jnp.zeros_like(prev_ref)

    rows = lax.broadcasted_iota(jnp.int32, (tm, 1), 0)
    p1 = prev_ref[7:8, :]
    p2 = prev_ref[6:7, :]
    u1 = jnp.where(rows == 0, p1, pltpu.roll(u, 1, axis=0))
    u2 = jnp.where(rows == 0, p2, jnp.where(rows == 1, p1, pltpu.roll(u, 2, axis=0)))
    prev_ref[...] = u[tm - 8:tm, :]
    y = proj[:, 0:c] * (cw_ref[0:1, :] * u2 + cw_ref[1:2, :] * u1 + cw_ref[2:3, :] * u)
    ms = _split_dot(y * y, gmat_ref[...])
    yc_ref[0] = (y * lax.rsqrt(ms + EPS) * gc_ref[...]).astype(BF16)
    o = 3 * c
    q_ref[0] = (proj[:, o:o + SB_DIM] * (SB_HEAD_DIM ** -0.5)).astype(BF16)
    k_ref[0] = proj[:, o + SB_DIM:o + 2 * SB_DIM].astype(BF16)
    v_ref[0] = proj[:, o + 2 * SB_DIM:o + 3 * SB_DIM].astype(BF16)


def _inproj(x, g_mix, w_in, conv_w, g_conv, gmat):
    b, s, d = x.shape
    tm = ROW_TILE
    p = w_in.shape[1]
    row = lambda i, j: (i, j, 0)
    fixed = lambda i, j: (0, 0)
    outs = [jax.ShapeDtypeStruct((b, s, CONV_DIM), BF16)] * 4
    return pl.pallas_call(
        _inproj_kernel,
        grid=(b, s // tm),
        in_specs=[pl.BlockSpec((1, tm, d), row),
                  pl.BlockSpec((1, d), fixed),
                  pl.BlockSpec((d, p), fixed),
                  pl.BlockSpec(conv_w.shape, fixed),
                  pl.BlockSpec((1, CONV_DIM), fixed),
                  pl.BlockSpec((CONV_DIM, CONV_DIM), fixed)],
        out_specs=[pl.BlockSpec((1, tm, CONV_DIM), row)] * 4,
        out_shape=outs,
        scratch_shapes=[pltpu.VMEM((8, CONV_DIM), F32)],
        compiler_params=pltpu.CompilerParams(
            dimension_semantics=("arbitrary", "arbitrary"), vmem_limit_bytes=VMEM_LIMIT),
        name="inproj_conv",
    )(x, g_mix, w_in, conv_w, g_conv, gmat)


def _sb_kernel(q_ref, k_ref, v_ref, tri_ref, g_ref, o_ref):
    tq = tri_ref.shape[1]
    tk = tq
    lane = lax.broadcasted_iota(jnp.int32, (1, LANES), 1)
    in_a = lane < SB_HEAD_DIM
    tri = tri_ref[...]

    def scores(qh, kb):
        return lax.dot_general(qh, kb, (((1,), (1,)), ((), ())), preferred_element_type=F32) * LOG2E

    def later_keys(z2, valid):
        sp2 = jnp.maximum(z2, 0.0) + jnp.log2(1.0 + jnp.exp2(-jnp.abs(z2)))
        if valid is not None:
            sp2 = jnp.where(valid, sp2, 0.0)
        hi = sp2.astype(BF16)
        lo = (sp2 - hi.astype(F32)).astype(BF16)
        rest = jnp.dot(jnp.concatenate([hi, lo], axis=1), tri,
                       preferred_element_type=F32)
        return sp2, rest

    def weights(z2, sp2, rest, run, valid):
        a = jnp.exp2(z2 - sp2 - rest - run)
        if valid is not None:
            a = jnp.where(valid, a, 0.0)
        return a.astype(BF16), run + rest[:, 0:1] + sp2[:, 0:1]

    def tiles(q_heads, js, carry, valids):
        run = [carry[0], carry[2]]
        acc = [carry[1], carry[3]]
        kbs, vbs = [], []
        for j in js:
            start = pl.multiple_of(j * tk, tk)
            kbs.append(k_ref[0, pl.ds(start, tk), :])
            vbs.append(v_ref[0, pl.ds(start, tk), :])
        zs = [[scores(qh, kb) for qh in q_heads] for kb in kbs]
        mids = [[later_keys(z, valid) for z in zt] for zt, valid in zip(zs, valids)]
        for t in range(len(js)):
            for h in range(2):
                w, run[h] = weights(zs[t][h], *mids[t][h], run[h], valids[t])
                acc[h] = acc[h] + jnp.dot(w, vbs[t], preferred_element_type=F32)
        return run[0], acc[0], run[1], acc[1]

    r0 = jnp.zeros((tq, 1), F32)
    a0 = jnp.zeros((tq, LANES), F32)
    t_idx = lax.broadcasted_iota(jnp.int32, (tq, tk), 0)
    s_idx = lax.broadcasted_iota(jnp.int32, (tq, tk), 1)

    def query_tile(qi, _):
        rows = pl.ds(pl.multiple_of(qi * tq, tq), tq)
        q = q_ref[0, rows, :]
        zero = jnp.zeros_like(q)
        q_heads = (jnp.where(in_a, q, zero), jnp.where(in_a, zero, q))
        diag = s_idx < t_idx
        start = (r0, a0, r0, a0)
        odd = qi % 2
        carry = lax.cond(odd == 1,
                         lambda c: tiles(q_heads, [qi, qi - 1], c, [diag, None]),
                         lambda c: tiles(q_heads, [qi], c, [diag]), start)
        first = qi - 1 - odd
        carry = lax.fori_loop(
            0, qi // 2,
            lambda n, c: tiles(q_heads, [first - 2 * n, first - 2 * n - 1], c, [None, None]), carry)
        out = jnp.where(in_a, carry[1], carry[3])
        sq = out * out
        ss_a = jnp.sum(jnp.where(in_a, sq, 0.0), axis=-1, keepdims=True)
        ss_b = jnp.sum(jnp.where(in_a, 0.0, sq), axis=-1, keepdims=True)
        ms = jnp.where(in_a, ss_a, ss_b) * (1.0 / SB_HEAD_DIM)
        o_ref[0, rows, :] = (out * lax.rsqrt(ms + EPS) * g_ref[...]).astype(BF16)
        return 0

    lax.fori_loop(0, q_ref.shape[1] // tq, query_tile, 0)


def _stick_breaking(q, k, v, tri, g_sb):
    b, s, w = q.shape
    tq = SB_TILE
    return pl.pallas_call(
        _sb_kernel,
        grid=(b, w // LANES),
        in_specs=[pl.BlockSpec((1, s, LANES), lambda i, h: (i, 0, h)),
                  pl.BlockSpec((1, s, LANES), lambda i, h: (i, 0, h)),
                  pl.BlockSpec((1, s, LANES), lambda i, h: (i, 0, h)),
                  pl.BlockSpec((2 * tq, tq), lambda i, h: (0, 0)),
                  pl.BlockSpec((1, LANES), lambda i, h: (0, h))],
        out_specs=pl.BlockSpec((1, s, LANES), lambda i, h: (i, 0, h)),
        out_shape=jax.ShapeDtypeStruct((b, s, w), BF16),
        compiler_params=pltpu.CompilerParams(
            dimension_semantics=("arbitrary", "arbitrary"),
            vmem_limit_bytes=VMEM_LIMIT),
        name="stick_breaking",
    )(q, k, v, tri, g_sb)


def _mid_kernel(x_ref, yc_ref, ys_ref, wo1_ref, wo2_ref, gx_ref, wq_ref, kv_ref, wom_ref,
                gm_ref, wr_ref, br_ref, ltri_ref,
                x2_ref, h3_ref, route_ref, cnt_ref, carry_ref):
    tm = x_ref.shape[1]
    d = x_ref.shape[2]
    hd = d // X_HEADS

    @pl.when((pl.program_id(0) == 0) & (pl.program_id(1) == 0))
    def _():
        carry_ref[...] = jnp.zeros_like(carry_ref)

    x1 = (x_ref[0]
          + jnp.dot(yc_ref[0], wo1_ref[...], preferred_element_type=F32)
          + jnp.dot(ys_ref[0], wo2_ref[...], preferred_element_type=F32))
    h2 = _rms(x1, gx_ref[...]).astype(BF16)
    qm = (jnp.dot(h2, wq_ref[...], preferred_element_type=F32) * (hd ** -0.5)).astype(BF16)
    heads = []
    for h in range(X_HEADS):
        kh = kv_ref[0, :, h * hd:(h + 1) * hd]
        vh = kv_ref[0, :, d + h * hd:d + (h + 1) * hd]
        sc = lax.dot_general(qm[:, h * hd:(h + 1) * hd], kh, (((1,), (1,)), ((), ())),
                             preferred_element_type=F32)
        e = jnp.exp(sc - jnp.max(sc, axis=-1, keepdims=True))
        p = e / jnp.sum(e, axis=-1, keepdims=True)
        heads.append(jnp.dot(p.astype(BF16), vh, preferred_element_type=F32).astype(BF16))
    x2 = x1 + jnp.dot(jnp.concatenate(heads, axis=-1), wom_ref[...], preferred_element_type=F32)
    x2_ref[0] = x2
    h3 = _rms(x2, gm_ref[...])
    h3_ref[0] = h3

    h_hi = h3.astype(BF16)
    h_lo = (h3 - h_hi.astype(F32)).astype(BF16)
    logits = jnp.dot(jnp.concatenate([h_hi, h_lo, h_hi], axis=1), wr_ref[...],
                     preferred_element_type=F32) + br_ref[...]
    lane = lax.broadcasted_iota(jnp.int32, (tm, N_EXPERTS), 1).astype(F32)
    vals, idxs = [], []
    l = logits
    for _ in range(TOP_K):
        m = jnp.max(l, axis=-1, keepdims=True)
        idx = jnp.min(jnp.where(l == m, lane, float(N_EXPERTS)), axis=-1, keepdims=True)
        vals.append(m)
        idxs.append(idx)
        l = jnp.where(lane == idx, -jnp.inf, l)
    es = [jnp.exp(vk - vals[0]) for vk in vals]
    den = es[0] + es[1] + es[2] + es[3]
    gates = [ek / den for ek in es]

    sel = jnp.zeros((tm, N_EXPERTS), F32)
    for idx in idxs:
        sel = sel + (lane == idx).astype(F32)
    before = jnp.dot(ltri_ref[...], sel.astype(BF16), preferred_element_type=F32) + carry_ref[...]
    ranks = [jnp.sum(jnp.where(lane == idx, before, 0.0), axis=-1, keepdims=True) for idx in idxs]
    carry_ref[...] += jnp.sum(sel, axis=0, keepdims=True)
    cnt_ref[...] = carry_ref[...]

    lane_o = lax.broadcasted_iota(jnp.int32, (tm, LANES), 1)
    r = jnp.zeros((tm, LANES), F32)
    for k in range(TOP_K):
        r = jnp.where(lane_o == k, idxs[k], r)
        r = jnp.where(lane_o == TOP_K + k, gates[k], r)
        r = jnp.where(lane_o == 2 * TOP_K + k, ranks[k], r)
    route_ref[0] = r


def _mid(x, yc, ys, wo1, wo2, g_x, w_q, kv, w_om, g_moe, w_r, b_r, ltri):
    b, s, d = x.shape
    tm = ROW_TILE
    m = kv.shape[1]
    row = lambda i, j: (i, j, 0)
    fixed = lambda i, j: (0, 0)
    return pl.pallas_call(
        _mid_kernel,
        grid=(b, s // tm),
        in_specs=[pl.BlockSpec((1, tm, d), row),
                  pl.BlockSpec((1, tm, CONV_DIM), row),
                  pl.BlockSpec((1, tm, SB_DIM), row),
                  pl.BlockSpec((CONV_DIM, d), fixed),
                  pl.BlockSpec((SB_DIM, d), lambda i, j: (CONV_DIM // SB_DIM, 0)),
                  pl.BlockSpec((1, d), fixed),
                  pl.BlockSpec((d, d), fixed),
                  pl.BlockSpec((1, m, 2 * d), lambda i, j: (i, 0, 0)),
                  pl.BlockSpec((d, d), fixed),
                  pl.BlockSpec((1, d), fixed),
                  pl.BlockSpec((3 * d, N_EXPERTS), fixed),
                  pl.BlockSpec((1, N_EXPERTS), fixed),
                  pl.BlockSpec((tm, tm), fixed)],
        out_specs=[pl.BlockSpec((1, tm, d), row),
                   pl.BlockSpec((1, tm, d), row),
                   pl.BlockSpec((1, tm, LANES), row),
                   pl.BlockSpec((1, N_EXPERTS), fixed)],
        out_shape=[jax.ShapeDtypeStruct((b, s, d), F32),
                   jax.ShapeDtypeStruct((b, s, d), F32),
                   jax.ShapeDtypeStruct((b, s, LANES), F32),
                   jax.ShapeDtypeStruct((1, N_EXPERTS), F32)],
        scratch_shapes=[pltpu.VMEM((1, N_EXPERTS), F32)],
        compiler_params=pltpu.CompilerParams(
            dimension_semantics=("arbitrary", "arbitrary"), vmem_limit_bytes=VMEM_LIMIT),
        name="outproj_xattn_router",
    )(x, yc, ys, wo1, wo2, g_x, w_q, kv, w_om, g_moe, w_r, b_r, ltri)


def _row_copy(src, src_row, dst, dst_row, sem):
    return pltpu.make_async_copy(src.at[pl.ds(src_row, 1), :], dst.at[pl.ds(dst_row, 1), :], sem)


def _dispatch_kernel(cnt_ref, pad_ref, start_ref, dest_ref, h_ref, zero_ref, xs_ref, stage_ref, sem, zsem):
    i = pl.program_id(0)
    n_steps = pl.num_programs(0)
    tm = h_ref.shape[0]
    te = zero_ref.shape[0]
    slot = i % 2

    def block_copy(row):
        rows = pl.ds(pl.multiple_of(row, te), te)
        return pltpu.make_async_copy(zero_ref, xs_ref.at[rows, :], zsem)

    @pl.when(i == 0)
    def _():
        last = N_EXPERTS - 1
        n_used = (start_ref[last] + pad_ref[last]) // te
        n_blocks = xs_ref.shape[0] // te

        def fill(wait):
            def expert(e, c):
                @pl.when(pad_ref[e] > cnt_ref[e])
                def _():
                    cp = block_copy(start_ref[e] + pad_ref[e] - te)
                    cp.wait() if wait else cp.start()
                return c

            def tail(j, c):
                cp = block_copy(j * te)
                cp.wait() if wait else cp.start()
                return c

            lax.fori_loop(0, N_EXPERTS, expert, 0)
            lax.fori_loop(n_used, n_blocks, tail, 0)

        fill(False)
        fill(True)

    def drain(s):
        def body(g, c):
            rows = pl.ds(pl.multiple_of(g * WAIT_ROWS, WAIT_ROWS), WAIT_ROWS)
            for k in range(TOP_K):
                pltpu.make_async_copy(stage_ref.at[s, rows, :], xs_ref.at[pl.ds(0, WAIT_ROWS), :],
                                      sem.at[s]).wait()
            return c
        lax.fori_loop(0, tm // WAIT_ROWS, body, 0)

    @pl.when(i >= 2)
    def _():
        drain(slot)

    stage_ref[slot] = h_ref[...]

    def issue(t, c):
        for k in range(TOP_K):
            _row_copy(stage_ref.at[slot], t, xs_ref, dest_ref[t * TOP_K + k],
                      sem.at[slot]).start(priority=k % 2)
        return c

    lax.fori_loop(0, tm, issue, 0, unroll=8)

    @pl.when(i == n_steps - 1)
    def _():
        @pl.when(n_steps > 1)
        def _():
            drain(1 - slot)
        drain(slot)


def _dispatch(counts, padded, start, dest_flat, h3, n_rows):
    n, d = h3.shape
    tm = ROW_TILE
    zero = jnp.zeros((EXPERT_TILE, d), h3.dtype)
    return pl.pallas_call(
        _dispatch_kernel,
        grid_spec=pltpu.PrefetchScalarGridSpec(
            num_scalar_prefetch=3,
            grid=(n // tm,),
            in_specs=[pl.BlockSpec((tm * TOP_K,), lambda i, *_: (i,), memory_space=pltpu.SMEM),
                      pl.BlockSpec((tm, d), lambda i, *_: (i, 0)),
                      pl.BlockSpec(zero.shape, lambda i, *_: (0, 0))],
            out_specs=pl.BlockSpec(memory_space=pl.ANY),
            scratch_shapes=[pltpu.VMEM((2, tm, d), h3.dtype), pltpu.SemaphoreType.DMA((2,)),
                            pltpu.SemaphoreType.DMA]),
        out_shape=jax.ShapeDtypeStruct((n_rows, d), h3.dtype),
        compiler_params=pltpu.CompilerParams(
            dimension_semantics=("arbitrary",), vmem_limit_bytes=VMEM_LIMIT),
        name="moe_dispatch",
    )(counts, padded, start, dest_flat, h3, zero)


def _expert_kernel(be_ref, nu_ref, bv_ref, slot_ref, nxt_ref, xs_ref, w1_ref, b1_ref, w2_ref, b2_ref, o_ref,
                   w1c_ref, w2c_ref, w1f_ref, w2f_ref, wsem):
    i = pl.program_id(0)
    tm = xs_ref.shape[0]
    de = w2_ref.shape[1]

    def weight_copies(e, s):
        return (pltpu.make_async_copy(w1_ref.at[e], w1f_ref.at[s], wsem.at[s, 0]),
                pltpu.make_async_copy(w2_ref.at[e], w2f_ref.at[s], wsem.at[s, 1]))

    @pl.when(i >= nu_ref[0])
    def _():
        o_ref[...] = jnp.zeros_like(o_ref)

    @pl.when(i < nu_ref[0])
    def _():
        changed = (i == 0) | (be_ref[i] != be_ref[jnp.maximum(i - 1, 0)])

        @pl.when(changed)
        def _():
            e = be_ref[i]
            s = slot_ref[i]

            @pl.when(i == 0)
            def _():
                for cp in weight_copies(e, s):
                    cp.start()

            for cp in weight_copies(e, s):
                cp.wait()
            w1c_ref[...] = w1f_ref[s].astype(BF16)
            w2c_ref[...] = w2f_ref[s].astype(BF16)

            @pl.when(nxt_ref[i] >= 0)
            def _():
                for cp in weight_copies(nxt_ref[i], 1 - s):
                    cp.start()

        ch = EXPERT_CHUNK

        def activation(xb, c):
            g = (jnp.dot(xb, w1c_ref[:, c * ch:(c + 1) * ch], preferred_element_type=F32)
                 + b1_ref[0, :, c * ch:(c + 1) * ch])
            l = (jnp.dot(xb, w1c_ref[:, de + c * ch:de + (c + 1) * ch], preferred_element_type=F32)
                 + b1_ref[0, :, de + c * ch:de + (c + 1) * ch])
            glu = jnp.minimum(g, SWIGLU_LIMIT)
            lin = jnp.clip(l, -SWIGLU_LIMIT, SWIGLU_LIMIT)
            half = 0.5 * glu
            return ((half + half * jnp.tanh((0.5 * SWIGLU_ALPHA) * glu)) * (lin + 1.0)).astype(BF16)

        def mlp(rows):
            xb = xs_ref[0:rows, :].astype(BF16)
            out = b2_ref[0]
            act = activation(xb, 0)
            for c in range(de // ch):
                nxt = activation(xb, c + 1) if (c + 1) * ch < de else None
                out = out + jnp.dot(act, w2c_ref[c * ch:(c + 1) * ch, :], preferred_element_type=F32)
                act = nxt
            o_ref[0:rows, :] = out
            if rows < tm:
                o_ref[rows:tm, :] = jnp.zeros((tm - rows, o_ref.shape[1]), F32)

        @pl.when(bv_ref[i] > tm // 2)
        def _():
            mlp(tm)

        @pl.when(bv_ref[i] <= tm // 2)
        def _():
            mlp(tm // 2)


def _experts(block_e, n_used, block_valid, block_slot, block_next, xs, w1, b1, w2, b2):
    n_rows, dp = xs.shape
    tm = EXPERT_TILE
    ne, d, dh = w1.shape
    de = w2.shape[1]
    rows = lambda i, be, nu, *_: (jnp.maximum(jnp.minimum(i, nu[0] - 1), 0), 0)
    per_e = lambda i, be, *_: (be[i], 0, 0)
    return pl.pallas_call(
        _expert_kernel,
        grid_spec=pltpu.PrefetchScalarGridSpec(
            num_scalar_prefetch=5,
            grid=(n_rows // tm,),
            in_specs=[pl.BlockSpec((tm, dp), rows),
                      pl.BlockSpec(memory_space=pl.ANY),
                      pl.BlockSpec((1, 1, dh), per_e),
                      pl.BlockSpec(memory_space=pl.ANY),
                      pl.BlockSpec((1, 1, d), per_e)],
            out_specs=pl.BlockSpec((tm, d), lambda i, *_: (i, 0)),
            scratch_shapes=[pltpu.VMEM((d, dh), BF16), pltpu.VMEM((de, d), BF16),
                            pltpu.VMEM((2, d, dh), F32), pltpu.VMEM((2, de, d), F32),
                            pltpu.SemaphoreType.DMA((2, 2))]),
        out_shape=jax.ShapeDtypeStruct((n_rows, d), F32),
        compiler_params=pltpu.CompilerParams(
            dimension_semantics=("arbitrary",), vmem_limit_bytes=VMEM_LIMIT),
        name="moe_experts",
    )(block_e, n_used, block_valid, block_slot, block_next, xs, w1, b1.reshape(ne, 1, dh), w2,
      b2.reshape(ne, 1, d))


def _combine_kernel(dest_ref, x2_ref, route_ref, g_ref, ys_ref, o_ref, buf_ref, sem):
    tm = x2_ref.shape[0]
    half = tm // 2

    def gather(r0, s, wait):
        def body(t, c):
            for k in range(TOP_K):
                _row_copy(ys_ref, dest_ref[t * TOP_K + k], buf_ref.at[k], t, sem.at[s]).start(priority=k % 2)
            return c

        def wait_body(g, c):
            rows = pl.ds(pl.multiple_of(r0 + g * WAIT_ROWS, WAIT_ROWS), WAIT_ROWS)
            for k in range(TOP_K):
                pltpu.make_async_copy(ys_ref.at[pl.ds(0, WAIT_ROWS), :], buf_ref.at[k, rows, :],
                                      sem.at[s]).wait()
            return c

        if wait:
            lax.fori_loop(0, half // WAIT_ROWS, wait_body, 0)
        else:
            lax.fori_loop(r0, r0 + half, body, 0, unroll=8)

    def finish(r0):
        route = route_ref[r0:r0 + half, :]
        x3 = x2_ref[r0:r0 + half, :]
        for k in range(TOP_K):
            x3 = x3 + buf_ref[k, r0:r0 + half, :] * route[:, TOP_K + k:TOP_K + k + 1]
        o_ref[r0:r0 + half, :] = _rms(x3, g_ref[...])

    gather(0, 0, False)
    gather(half, 1, False)
    gather(0, 0, True)
    finish(0)
    gather(half, 1, True)
    finish(half)


def _combine(dest_flat, x2, route, g_final, ys):
    n, d = x2.shape
    tm = ROW_TILE
    return pl.pallas_call(
        _combine_kernel,
        grid=(n // tm,),
        in_specs=[pl.BlockSpec((tm * TOP_K,), lambda i: (i,), memory_space=pltpu.SMEM),
                  pl.BlockSpec((tm, d), lambda i: (i, 0)),
                  pl.BlockSpec((tm, LANES), lambda i: (i, 0)),
                  pl.BlockSpec((1, d), lambda i: (0, 0)),
                  pl.BlockSpec(memory_space=pl.ANY)],
        out_specs=pl.BlockSpec((tm, d), lambda i: (i, 0)),
        out_shape=jax.ShapeDtypeStruct((n, d), F32),
        scratch_shapes=[pltpu.VMEM((TOP_K, tm, d), F32), pltpu.SemaphoreType.DMA((2,))],
        compiler_params=pltpu.CompilerParams(
            dimension_semantics=("arbitrary",), vmem_limit_bytes=VMEM_LIMIT),
        name="moe_combine_norm",
    )(dest_flat, x2, route, g_final, ys)


def _layer(x, mem, g_mix, w_in, conv_w, g_conv_out, g_sb_out, w_out, g_xattn, g_mem,
           w_q_mem, w_kv_mem, w_o_mem, g_moe, w_router, b_router, w1, b1, w2, b2):
    b, s, d = x.shape
    n = b * s
    row2 = lambda g: g.reshape(1, -1)

    grp = jnp.arange(CONV_DIM) // CONV_GROUP_DIM
    gmat = jnp.where(grp[:, None] == grp[None, :], 1.0 / CONV_GROUP_DIM, 0.0).astype(BF16)
    ar = jnp.arange(SB_TILE)
    tri = (ar[:, None] > ar[None, :]).astype(BF16)
    tri = jnp.concatenate([tri, tri], axis=0)
    ar = jnp.arange(ROW_TILE)
    ltri = (ar[:, None] > ar[None, :]).astype(BF16)

    kv = _mem_kv(mem, row2(g_mem), w_kv_mem.astype(BF16))
    yc, q, k, v = _inproj(x, row2(g_mix), w_in.astype(BF16), conv_w, row2(g_conv_out), gmat)
    ys = _stick_breaking(q, k, v, tri, row2(g_sb_out))
    w_out_b = w_out.astype(BF16)
    wr_hi = w_router.astype(BF16)
    wr_lo = (w_router - wr_hi.astype(F32)).astype(BF16)
    x2, h3, route, counts = _mid(x, yc, ys, w_out_b, w_out_b, row2(g_xattn),
                                 w_q_mem.astype(BF16), kv, w_o_mem.astype(BF16), row2(g_moe),
                                 jnp.concatenate([wr_hi, wr_hi, wr_lo], axis=0), row2(b_router), ltri)

    te = EXPERT_TILE
    route = route.reshape(n, LANES)
    counts = counts.reshape(N_EXPERTS).astype(jnp.int32)
    padded = (counts + te - 1) // te * te
    padded_end = jnp.cumsum(padded)
    start = padded_end - padded
    top_i = route[:, 0:TOP_K].astype(jnp.int32)
    rank = route[:, 2 * TOP_K:3 * TOP_K].astype(jnp.int32)
    experts = jnp.arange(N_EXPERTS, dtype=jnp.int32)
    first = jnp.sum(jnp.where(top_i[:, :, None] == experts, start, 0), axis=-1)
    dest = (first + rank).reshape(n * TOP_K)
    n_rows = n * TOP_K + N_EXPERTS * te
    n_blocks = n_rows // te
    n_used = (padded_end[-1] // te).astype(jnp.int32)
    blk = jnp.minimum(jnp.arange(n_blocks, dtype=jnp.int32), n_used - 1) * te
    block_e = jnp.minimum(jnp.sum((padded_end[None, :] <= blk[:, None]).astype(jnp.int32), axis=1),
                          N_EXPERTS - 1)

    xs = _dispatch(counts, padded, start, dest, h3.reshape(n, d), n_rows)
    block_valid = jnp.sum(jnp.where(block_e[:, None] == experts, counts - (blk[:, None] - start), 0), axis=1)
    nonempty = counts > 0
    slot_e = (jnp.cumsum(nonempty.astype(jnp.int32)) - 1) % 2
    later = (experts[None, :] > experts[:, None]) & nonempty[None, :]
    next_e = jnp.min(jnp.where(later, experts[None, :], N_EXPERTS), axis=1)
    next_e = jnp.where(next_e < N_EXPERTS, next_e, -1)
    of_block = block_e[:, None] == experts
    block_slot = jnp.sum(jnp.where(of_block, slot_e, 0), axis=1).astype(jnp.int32)
    block_next = jnp.sum(jnp.where(of_block, next_e, 0), axis=1).astype(jnp.int32)
    yo = _experts(block_e, n_used.reshape(1), block_valid.astype(jnp.int32), block_slot, block_next,
                  xs, w1, b1, w2, b2)
    return dest, x2.reshape(n, d), route, yo


def kernel(x, mem, g_mix, w_in, conv_w, g_conv_out, g_sb_out, w_out, g_xattn, g_mem, w_q_mem,
           w_kv_mem, w_o_mem, g_moe, w_router, b_router, w1, b1, w2, b2, g_final):
    b, s, d = x.shape
    assert w_in.shape[0] == 1, "single-layer stack"
    dest, x2, route, yo = _layer(x, mem, g_mix[0], w_in[0], conv_w[0], g_conv_out[0], g_sb_out[0],
                                 w_out[0], g_xattn[0], g_mem[0], w_q_mem[0], w_kv_mem[0],
                                 w_o_mem[0], g_moe[0], w_router[0], b_router[0],
                                 w1[0], b1[0], w2[0], b2[0])
    out = _combine(dest, x2, route, g_final.reshape(1, d), yo)
    return out.reshape(b, s, d)
```
